```python
import jax, jax.numpy as jnp
from jax import lax
import numpy as np

D_MODEL = 1024
BATCH = 8
SEQ = 2048
DEPTH = 2

GRID_W = 64
MIX_WIDTH = D_MODEL
HG_WIDTH = MIX_WIDTH // 2
HG_HEAD_DIM = 128
HG_HEADS = HG_WIDTH // HG_HEAD_DIM
HG_CHUNK = 64
ATT_WIDTH = MIX_WIDTH - HG_WIDTH
ATT_HEAD_DIM = 64
ATT_HEADS = ATT_WIDTH // ATT_HEAD_DIM
ATT_KV_HEADS = 2
ATT_GROUP = ATT_HEADS // ATT_KV_HEADS
KV_WIDTH = ATT_KV_HEADS * ATT_HEAD_DIM
Q_BLOCK = 128
ROPE_THETA = 10000.0
D_FF = 4 * D_MODEL
N_MOD = 6
EPS = 1e-6
IN_SIZES = [HG_WIDTH] * 5 + [ATT_WIDTH, KV_WIDTH, KV_WIDTH]
IN_COLS = sum(IN_SIZES)
IN_SPLITS = [int(s) for s in np.cumsum(IN_SIZES)[:-1]]

kernel_name = 'hybrid_hgrn2_axial_gqa_encoder'

F32 = jnp.float32


def rms_norm(x, gain):
    xf = x.astype(F32)
    y = xf * lax.rsqrt(jnp.mean(xf * xf, axis=-1, keepdims=True) + EPS)
    return (y * gain.astype(F32)).astype(x.dtype)


def hgrn_lower_bounds(lb_logits):
    p = jnp.cumsum(jax.nn.softmax(lb_logits.astype(F32), axis=0), axis=0)
    return p - p[0:1]


def chunked_gated_recurrence(q, k, v, log_f):
    N, S, H, dk = q.shape
    dv = v.shape[-1]
    nc = S // HG_CHUNK

    def to_chunks(a):
        return a.reshape(N, nc, HG_CHUNK, H, a.shape[-1]).transpose(1, 0, 3, 2, 4)

    xs = (to_chunks(q), to_chunks(k), to_chunks(v), to_chunks(log_f))
    lower = jnp.tril(jnp.ones((HG_CHUNK, HG_CHUNK), dtype=bool))[:, :, None]

    def step(state, inp):
        qi, ki, vi, gi = inp
        b = jnp.cumsum(gi, axis=-2)
        b_last = b[..., -1:, :]
        o_inter = jnp.einsum('nhtd,nhde->nhte', qi * jnp.exp(b), state)
        diff = b[..., :, None, :] - b[..., None, :, :]
        decay = jnp.exp(jnp.where(lower, diff, -jnp.inf))
        scores = jnp.einsum('nhtd,nhsd,nhtsd->nhts', qi, ki, decay)
        o_intra = jnp.einsum('nhts,nhse->nhte', scores, vi)
        k_dec = ki * jnp.exp(b_last - b)
        new_state = state * jnp.exp(b_last)[..., 0, :, None] + jnp.einsum('nhsd,nhse->nhde', k_dec, vi)
        return new_state, o_inter + o_intra

    state0 = jnp.zeros((N, H, dk, dv), F32)
    _, o = lax.scan(step, state0, xs)
    return o.transpose(1, 0, 3, 2, 4).reshape(N, S, H, dv)


def hgrn2_group(q, f_fwd, f_bwd, i, g, lb, out_gain):
    B, S, _ = q.shape

    def heads(a):
        return a.astype(F32).reshape(B, S, HG_HEADS, HG_HEAD_DIM)

    qh = jax.nn.silu(heads(q))
    ih = heads(i)
    lbh = lb.reshape(HG_HEADS, HG_HEAD_DIM)

    def gates(fz):
        z = heads(fz)
        log_f = jnp.logaddexp(jnp.log(lbh), jnp.log1p(-lbh) + jax.nn.log_sigmoid(z))
        return log_f, (1.0 - lbh) * jax.nn.sigmoid(-z)

    lf_fw, k_fw = gates(f_fwd)
    lf_bw, k_bw = gates(f_bwd)
    flip = lambda a: a[:, ::-1]
    both = lambda a, b: jnp.concatenate([a, flip(b)], axis=0)
    o = chunked_gated_recurrence(both(qh, qh), both(k_fw, k_bw), both(ih, ih), both(lf_fw, lf_bw))
    o = o[:B] + flip(o[B:])
    o = rms_norm(o, out_gain) * jax.nn.silu(heads(g))
    return o.reshape(B, S, HG_WIDTH).astype(q.dtype)


def axial_angles(seq_len):
    rows = seq_len // GRID_W
    row = jnp.repeat(jnp.arange(rows, dtype=F32), GRID_W)
    col = jnp.tile(jnp.arange(GRID_W, dtype=F32), rows)
    half = ATT_HEAD_DIM // 2
    inv_freq = 1.0 / (ROPE_THETA ** (jnp.arange(0, half, 2, dtype=F32) / half))
    return row[:, None, None] * inv_freq, col[:, None, None] * inv_freq


def rotate(x, ang):
    x1, x2 = jnp.split(x, 2, axis=-1)
    cos, sin = jnp.cos(ang), jnp.sin(ang)
    return jnp.concatenate([x1 * cos - x2 * sin, x2 * cos + x1 * sin], axis=-1)


def apply_axial_rope(x, ang_row, ang_col):
    xr, xc = jnp.split(x.astype(F32), 2, axis=-1)
    return jnp.concatenate([rotate(xr, ang_row), rotate(xc, ang_col)], axis=-1).astype(x.dtype)


def gqa_axial_group(q, k, v, q_gain, k_gain, out_gain):
    B, S, _ = q.shape
    qh = rms_norm(q.reshape(B, S, ATT_HEADS, ATT_HEAD_DIM), q_gain)
    kh = rms_norm(k.reshape(B, S, ATT_KV_HEADS, ATT_HEAD_DIM), k_gain)
    vh = v.reshape(B, S, ATT_KV_HEADS, ATT_HEAD_DIM)
    ang_r, ang_c = axial_angles(S)
    qh = apply_axial_rope(qh, ang_r, ang_c)
    kh = apply_axial_rope(kh, ang_r, ang_c)
    scale = ATT_HEAD_DIM ** -0.5
    nb = S // Q_BLOCK
    qb = qh.reshape(B, nb, Q_BLOCK, ATT_KV_HEADS, ATT_GROUP, ATT_HEAD_DIM).transpose(1, 0, 3, 4, 2, 5)
    kt = kh.transpose(0, 2, 1, 3)
    vt = vh.transpose(0, 2, 1, 3)

    def block(qi):
        s = jnp.einsum('bkgqd,bksd->bkgqs', qi, kt).astype(F32) * scale
        p = jax.nn.softmax(s, axis=-1).astype(vt.dtype)
        return jnp.einsum('bkgqs,bksd->bkgqd', p, vt)

    o = lax.map(block, qb)
    o = o.transpose(1, 0, 4, 2, 3, 5).reshape(B, S, ATT_HEADS, ATT_HEAD_DIM)
    o = rms_norm(o, out_gain)
    return o.reshape(B, S, ATT_WIDTH)


def hybrid_layer(x, mod, lb, pre_mix, post_mix, w_in, w_out, hg_out_gain, q_gain, k_gain,
                 att_out_gain, pre_ff, post_ff, w_ff1, w_ff2):
    shift1, scale1, gate1, shift2, scale2, gate2 = [m[:, None, :] for m in jnp.split(mod, N_MOD, axis=-1)]
    h = rms_norm(x, pre_mix) * (1.0 + scale1) + shift1
    z = h @ w_in
    hq, hf_fw, hf_bw, hi, hg, aq, ak, av = jnp.split(z, IN_SPLITS, axis=-1)
    y_hgrn = hgrn2_group(hq, hf_fw, hf_bw, hi, hg, lb, hg_out_gain)
    y_att = gqa_axial_group(aq, ak, av, q_gain, k_gain, att_out_gain)
    y = jnp.concatenate([y_hgrn, y_att.astype(y_hgrn.dtype)], axis=-1) @ w_out
    x = x + gate1 * rms_norm(y, post_mix)
    h = rms_norm(x, pre_ff) * (1.0 + scale2) + shift2
    y = jnp.square(jax.nn.relu(h @ w_ff1)) @ w_ff2
    return x + gate2 * rms_norm(y, post_ff)


def setup_inputs(seed: int = 0) -> dict:
    key = jax.random.key(seed)
    ks = jax.random.split(key, 20)
    nrm = lambda k, shape, s: jax.random.normal(k, shape, F32) * s
    gain = lambda k, shape: 1.0 + 0.05 * jax.random.normal(k, shape, F32)
    return {
        'x': nrm(ks[0], (BATCH, SEQ, D_MODEL), 1.0),
        'c': nrm(ks[1], (BATCH, D_MODEL), 1.0),
        'w_ada': nrm(ks[2], (D_MODEL, N_MOD * D_MODEL), D_MODEL ** -0.5),
        'ada_layer_bias': nrm(ks[3], (DEPTH, N_MOD * D_MODEL), 0.1),
        'hg_lb_logits': nrm(ks[4], (DEPTH, HG_WIDTH), 1.0),
        'pre_mix': gain(ks[5], (DEPTH, D_MODEL)),
        'post_mix': gain(ks[6], (DEPTH, D_MODEL)),
        'w_in': nrm(ks[7], (DEPTH, D_MODEL, IN_COLS), D_MODEL ** -0.5),
        'w_out': nrm(ks[8], (DEPTH, MIX_WIDTH, D_MODEL), MIX_WIDTH ** -0.5),
        'hg_out_gain': gain(ks[9], (DEPTH, HG_HEAD_DIM)),
        'q_gain': gain(ks[10], (DEPTH, ATT_HEAD_DIM)),
        'k_gain': gain(ks[11], (DEPTH, ATT_HEAD_DIM)),
        'att_out_gain': gain(ks[12], (DEPTH, ATT_HEAD_DIM)),
        'pre_ff': gain(ks[13], (DEPTH, D_MODEL)),
        'post_ff': gain(ks[14], (DEPTH, D_MODEL)),
        'w_ff1': nrm(ks[15], (DEPTH, D_MODEL, D_FF), D_MODEL ** -0.5),
        'w_ff2': nrm(ks[16], (DEPTH, D_FF, D_MODEL), D_FF ** -0.5),
    }


def reference(x, c, w_ada, ada_layer_bias, hg_lb_logits, pre_mix, post_mix, w_in, w_out,
              hg_out_gain, q_gain, k_gain, att_out_gain, pre_ff, post_ff, w_ff1, w_ff2):
    lb_table = hgrn_lower_bounds(hg_lb_logits)
    cond = jax.nn.silu(c) @ w_ada
    for l in range(DEPTH):
        mod = cond + ada_layer_bias[l]
        x = hybrid_layer(x, mod, lb_table[l], pre_mix[l], post_mix[l], w_in[l], w_out[l],
                         hg_out_gain[l], q_gain[l], k_gain[l], att_out_gain[l],
                         pre_ff[l], post_ff[l], w_ff1[l], w_ff2[l])
    return x
```

```python
import functools

import numpy as np
import jax
import jax.numpy as jnp
from jax import lax
from jax.experimental import pallas as pl
from jax.experimental.pallas import tpu as pltpu

F32 = jnp.float32
BF16 = jnp.bfloat16

LANES = 128
VMEM_LIMIT_BYTES = 56 * 1024 * 1024

EPS = 1e-6
N_MOD = 6
HG_HEADS = 4
HG_HEAD_DIM = 128
HG_WIDTH = HG_HEADS * HG_HEAD_DIM
HG_CHUNK = 64
HG_LEVEL_HALVES = (32, 16, 8, 4, 2, 1)
ATT_HEADS = 8
ATT_HEAD_DIM = 64
ATT_KV_HEADS = 2
ATT_WIDTH = ATT_HEADS * ATT_HEAD_DIM
KV_WIDTH = ATT_KV_HEADS * ATT_HEAD_DIM
GRID_W = 64
ROPE_THETA = 10000.0

CB_HQ, CB_FFW, CB_FBW, CB_HI, CB_HG = 0, 4, 8, 12, 16
CB_AQ, CB_AK, CB_AV = 20, 24, 25
N_COL_BLOCKS = 26
ATT_PAIRS = ATT_WIDTH // LANES
PAIRS_PER_KV = ATT_PAIRS // ATT_KV_HEADS


def _dot(a, b):
    return jnp.dot(a, b, preferred_element_type=F32)


def _dot_nt(a, b):
    return lax.dot_general(a, b, (((1,), (1,)), ((), ())), preferred_element_type=F32)


def _split2(x):
    hi = x.astype(BF16)
    lo = (x - hi.astype(F32)).astype(BF16)
    return hi, lo


def _split3(x):
    hi = x.astype(BF16)
    r1 = x - hi.astype(F32)
    mid = r1.astype(BF16)
    lo = (r1 - mid.astype(F32)).astype(BF16)
    return hi, mid, lo


def _sigmoid(x):
    t = jnp.exp(-jnp.abs(x))
    return jnp.where(x >= 0, 1.0, t) / (1.0 + t)


def _row_rms(x, gain):
    ms = jnp.mean(x * x, axis=-1, keepdims=True)
    return x * lax.rsqrt(ms + EPS) * gain


def _group_mean_sq(x, gmat):
    hi, lo = _split2(x * x)
    return _dot(hi, gmat) + _dot(lo, gmat)


def _cond_kernel(c_ref, w_ref, bias_ref, lbl_ref, mod_ref, lb_ref):
    depth = bias_ref.shape[0]
    c = c_ref[...]
    s = c * _sigmoid(c)
    s_hi = s.astype(BF16).astype(F32)
    s_lo = s - s_hi
    w_hi, w_lo = _split2(w_ref[...])
    nb = c.shape[0]
    r1 = _dot(jnp.concatenate([s_hi, s_lo], axis=0).astype(BF16), w_hi)
    r2 = _dot(jnp.concatenate([s_hi, s_hi], axis=0).astype(BF16), w_lo)
    cond = r1[:nb] + r1[nb:] + r2[:nb]
    for l in range(depth):
        mod_ref[l] = cond + bias_ref[l:l + 1, :]

    rows = [lbl_ref[l:l + 1, :] for l in range(depth)]
    m = functools.reduce(jnp.maximum, rows)
    es = [jnp.exp(r - m) for r in rows]
    tot = functools.reduce(lambda a, b: a + b, es)
    cum = None
    first = None
    for l in range(depth):
        p = es[l] / tot
        cum = p if cum is None else cum + p
        if first is None:
            first = cum
        lb_ref[l:l + 1, :] = cum - first


def _cond_call(c, w_ada, ada_layer_bias, hg_lb_logits):
    nb, d = c.shape
    n_out = w_ada.shape[1]
    depth = ada_layer_bias.shape[0]
    tn = n_out // 4
    return pl.pallas_call(
        _cond_kernel,
        grid=(n_out // tn,),
        in_specs=[
            pl.BlockSpec((nb, d), lambda j: (0, 0)),
            pl.BlockSpec((d, tn), lambda j: (0, j)),
            pl.BlockSpec((depth, tn), lambda j: (0, j)),
            pl.BlockSpec(hg_lb_logits.shape, lambda j: (0, 0)),
        ],
        out_specs=[
            pl.BlockSpec((depth, nb, tn), lambda j: (0, 0, j)),
            pl.BlockSpec(hg_lb_logits.shape, lambda j: (0, 0)),
        ],
        out_shape=[
            jax.ShapeDtypeStruct((depth, nb, n_out), F32),
            jax.ShapeDtypeStruct(hg_lb_logits.shape, F32),
        ],
        compiler_params=pltpu.CompilerParams(
            dimension_semantics=("arbitrary",), vmem_limit_bytes=VMEM_LIMIT_BYTES),
        name="cond",
    )(c, w_ada, ada_layer_bias, hg_lb_logits)


def _inproj_kernel(x_ref, mod_ref, gain_ref, w_ref, z_ref):
    x = x_ref[...]
    shift = mod_ref[0:1, :]
    scale = mod_ref[1:2, :]
    h = _row_rms(x, gain_ref[...]) * (1.0 + scale) + shift
    hb = h.astype(BF16)
    n_blocks = z_ref.shape[0]
    for cb in range(0, n_blocks, 2):
        z = _dot(hb, w_ref[:, cb * LANES:(cb + 2) * LANES])
        z_ref[cb] = z[:, :LANES]
        z_ref[cb + 1] = z[:, LANES:]


def _inproj_call(x2d, mod_l, gain, w_in_bf, seq, tm):
    rows, d = x2d.shape
    n_cols = w_in_bf.shape[1]
    n_blocks = n_cols // LANES
    tiles_per_seq = seq // tm
    return pl.pallas_call(
        _inproj_kernel,
        grid=(rows // tm,),
        in_specs=[
            pl.BlockSpec((tm, d), lambda i: (i, 0)),
            pl.BlockSpec((None, N_MOD, d), lambda i: (i // tiles_per_seq, 0, 0)),
            pl.BlockSpec((1, d), lambda i: (0, 0)),
            pl.BlockSpec((d, n_cols), lambda i: (0, 0), pipeline_mode=pl.Buffered(1)),
        ],
        out_specs=pl.BlockSpec((n_blocks, tm, LANES), lambda i: (0, i, 0)),
        out_shape=jax.ShapeDtypeStruct((n_blocks, rows, LANES), F32),
        compiler_params=pltpu.CompilerParams(
            dimension_semantics=("arbitrary",), vmem_limit_bytes=VMEM_LIMIT_BYTES),
        name="inproj",
    )(x2d, mod_l, gain, w_in_bf)


def _hgrn_tables():
    c = HG_CHUNK
    t = np.arange(c)[:, None]
    s = np.arange(c)[None, :]
    tri = np.stack([(s <= t), (s >= t)]).astype(np.float32)
    fwd = []
    for half in HG_LEVEL_HALVES:
        same = (t // (2 * half)) == (s // (2 * half))
        t_low = (t // half) % 2 == 1
        s_up = (s // half) % 2 == 0
        fwd.append(same & t_low & s_up)
    fwd.append(t == s)
    fwd = np.stack(fwd).astype(np.float32)
    masks = np.stack([fwd, fwd.transpose(0, 2, 1)])
    return tri, masks


def _hgrn_kernel(q_ref, ffw_ref, fbw_ref, i_ref, g_ref, lb_ref, gain_ref, tri_ref, mask_ref,
                 y_ref, b_scr, st_scr, o_scr):
    seq = q_ref.shape[0]
    n_chunks = seq // HG_CHUNK
    c = HG_CHUNK
    lb = lb_ref[...]
    log_lb = jnp.log(lb)
    log_1m_lb = jnp.log1p(-lb)
    one_m_lb = 1.0 - lb
    row = lax.broadcasted_iota(jnp.int32, (c, LANES), 0)
    sub = lax.broadcasted_iota(jnp.int32, (8, LANES), 0)

    st_scr[...] = jnp.zeros(st_scr.shape, F32)

    def chunk(d, f_ref, r0):
        zq = q_ref[pl.ds(r0, c), :]
        qs = zq * _sigmoid(zq)
        zf = f_ref[pl.ds(r0, c), :]
        v = i_ref[pl.ds(r0, c), :]

        t = jnp.exp(-jnp.abs(zf))
        one_p_t = 1.0 + t
        log_sig = jnp.minimum(zf, 0.0) - jnp.log1p(t)
        bb = log_1m_lb + log_sig
        g = jnp.maximum(log_lb, bb) + jnp.log1p(jnp.exp(-jnp.abs(log_lb - bb)))
        k = one_m_lb * (jnp.where(zf >= 0, t, 1.0) / one_p_t)

        g3 = jnp.concatenate(_split3(g), axis=1)
        b3 = _dot(tri_ref[d], g3)
        b = b3[:, :LANES] + b3[:, LANES:2 * LANES] + b3[:, 2 * LANES:]
        b_scr[d] = b
        edge = c - 1 if d == 0 else 0
        b_edge = b_scr[d, edge:edge + 1, :]

        def brow(r, n):
            return jnp.broadcast_to(b_scr[d, r:r + 1, :], (n, LANES))

        st = st_scr[d]
        o = _dot_nt((qs * jnp.exp(b)).astype(BF16), st.astype(BF16))

        scores = _dot_nt(qs.astype(BF16), k.astype(BF16)) * mask_ref[d, len(HG_LEVEL_HALVES)]
        for lvl, half in enumerate(HG_LEVEL_HALVES):
            if half >= 4:
                bm = jnp.concatenate(
                    [brow(base + half, 2 * half) for base in range(0, c, 2 * half)], axis=0) \
                    if half < 32 else brow(half, c)
            elif half == 2:
                bm = jnp.concatenate(
                    [jnp.where(sub < 4, brow(base + 2, 8), brow(base + 6, 8))
                     for base in range(0, c, 8)], axis=0)
            else:
                bm = jnp.where((row & 1) == 0, pltpu.roll(b, c - 1, axis=0), b)
            e = jnp.exp(-jnp.abs(b - bm))
            q_rows = ((row & half) != 0) if d == 0 else ((row & half) == 0)
            xl = (jnp.where(q_rows, qs, k) * e).astype(BF16)
            scores = scores + _dot_nt(xl, xl) * mask_ref[d, lvl]
        o = o + _dot(scores.astype(BF16), v.astype(BF16))
        o_scr[d, pl.ds(r0, c), :] = o

        kd = k * jnp.exp(b_edge - b)
        st_scr[d] = st * jnp.exp(b_edge) + _dot(v.T.astype(BF16), kd.astype(BF16))

    def body(i, carry):
        chunk(0, ffw_ref, pl.multiple_of(i * c, c))
        chunk(1, fbw_ref, pl.multiple_of((n_chunks - 1 - i) * c, c))
        return carry

    lax.fori_loop(0, n_chunks, body, 0)

    slab = 256
    gain = gain_ref[...]

    def fin(i, carry):
        r0 = pl.multiple_of(i * slab, slab)
        o = o_scr[0, pl.ds(r0, slab), :] + o_scr[1, pl.ds(r0, slab), :]
        zg = g_ref[pl.ds(r0, slab), :]
        y = _row_rms(o, gain) * (zg * _sigmoid(zg))
        y_ref[pl.ds(r0, slab), :] = y.astype(y_ref.dtype)
        return carry

    lax.fori_loop(0, seq // slab, fin, 0)


def _hgrn_call(z4, lb_l, out_gain):
    _, nb, seq, _ = z4.shape
    tri, masks = _hgrn_tables()

    def zspec(cb0):
        return pl.BlockSpec((None, None, seq, LANES), lambda b, h: (cb0 + h, b, 0, 0))

    y = pl.pallas_call(
        _hgrn_kernel,
        grid=(nb, HG_HEADS),
        in_specs=[
            zspec(CB_HQ), zspec(CB_FFW), zspec(CB_FBW), zspec(CB_HI), zspec(CB_HG),
            pl.BlockSpec((None, 1, LANES), lambda b, h: (h, 0, 0)),
            pl.BlockSpec((1, LANES), lambda b, h: (0, 0)),
            pl.BlockSpec(tri.shape, lambda b, h: (0, 0, 0)),
            pl.BlockSpec(masks.shape, lambda b, h: (0, 0, 0, 0)),
        ],
        out_specs=pl.BlockSpec((None, None, seq, LANES), lambda b, h: (h, b, 0, 0)),
        out_shape=jax.ShapeDtypeStruct((HG_HEADS, nb, seq, LANES), BF16),
        scratch_shapes=[
            pltpu.VMEM((2, HG_CHUNK, LANES), F32),
            pltpu.VMEM((2, HG_HEAD_DIM, HG_HEAD_DIM), F32),
            pltpu.VMEM((2, seq, LANES), F32),
        ],
        compiler_params=pltpu.CompilerParams(
            dimension_semantics=("arbitrary", "arbitrary"), vmem_limit_bytes=VMEM_LIMIT_BYTES),
        name="hgrn",
    )(z4, z4, z4, z4, z4, lb_l.reshape(HG_HEADS, 1, LANES), out_gain.reshape(1, LANES),
      jnp.asarray(tri, BF16), jnp.asarray(masks, F32))
    return y.reshape(HG_HEADS, nb * seq, LANES)


def _rope_tables(seq):
    pos = jnp.arange(seq)
    row = (pos // GRID_W).astype(F32)
    col = (pos % GRID_W).astype(F32)
    half = ATT_HEAD_DIM // 2
    inv_freq = 1.0 / (ROPE_THETA ** (jnp.arange(0, half, 2, dtype=F32) / half))
    ang_r = row[:, None] * inv_freq
    ang_c = col[:, None] * inv_freq
    cos = jnp.concatenate([jnp.cos(ang_r)] * 2 + [jnp.cos(ang_c)] * 2, axis=-1)
    sin = jnp.concatenate([-jnp.sin(ang_r), jnp.sin(ang_r), -jnp.sin(ang_c), jnp.sin(ang_c)], axis=-1)
    reps = LANES // ATT_HEAD_DIM
    return jnp.tile(cos, (1, reps)), jnp.tile(sin, (1, reps))


def _rope(x, cos, sin, lane):
    quarter = ATT_HEAD_DIM // 4
    fwd = pltpu.roll(x, LANES - quarter, axis=1)
    bwd = pltpu.roll(x, quarter, axis=1)
    partner = jnp.where((lane & quarter) == 0, fwd, bwd)
    return x * cos + partner * sin


def _attn_kernel(q0_ref, q1_ref, q2_ref, q3_ref, k_ref, v_ref, cosq_ref, sinq_ref, cosk_ref, sink_ref,
                 qg_ref, kg_ref, og_ref, gmat_ref, y_ref, kext_scr, vext_scr):
    seq = k_ref.shape[0]
    tq = q0_ref.shape[0]
    gmat = gmat_ref[...]

    @pl.when(pl.program_id(1) == 0)
    def _prepare_kv():
        slab = min(512, seq)
        lane = lax.broadcasted_iota(jnp.int32, (slab, LANES), 1)
        first = lane < ATT_HEAD_DIM
        for r0 in range(0, seq, slab):
            rs = slice(r0, r0 + slab)
            k = k_ref[rs, :]
            kn = k * lax.rsqrt(_group_mean_sq(k, gmat) + EPS) * kg_ref[...]
            kr = _rope(kn, cosk_ref[rs, :], sink_ref[rs, :], lane)
            kr_sw = pltpu.roll(kr, ATT_HEAD_DIM, axis=1)
            kext_scr[0, rs, :] = jnp.where(first, kr, kr_sw).astype(BF16)
            kext_scr[1, rs, :] = jnp.where(first, kr_sw, kr).astype(BF16)
            v = v_ref[rs, :]
            v_sw = pltpu.roll(v, ATT_HEAD_DIM, axis=1)
            vext_scr[0, 0, rs, :] = jnp.where(first, v, 1.0).astype(BF16)
            vext_scr[0, 1, rs, :] = jnp.where(first, 1.0, v_sw).astype(BF16)
            vext_scr[1, 0, rs, :] = jnp.where(first, v_sw, 1.0).astype(BF16)
            vext_scr[1, 1, rs, :] = jnp.where(first, 1.0, v).astype(BF16)

    lane = lax.broadcasted_iota(jnp.int32, (tq, LANES), 1)
    first = lane < ATT_HEAD_DIM
    cosq = cosq_ref[...]
    sinq = sinq_ref[...]
    sm_scale = ATT_HEAD_DIM ** -0.5
    for pair, q_ref in enumerate((q0_ref, q1_ref, q2_ref, q3_ref)):
        kv = pair // PAIRS_PER_KV
        q = q_ref[...]
        qn = q * lax.rsqrt(_group_mean_sq(q, gmat) + EPS) * (qg_ref[...] * sm_scale)
        qr = _rope(qn, cosq, sinq, lane)
        halves = []
        for parity in range(2):
            keep = first if parity == 0 else jnp.logical_not(first)
            qm = jnp.where(keep, qr, 0.0).astype(BF16)
            s = _dot_nt(qm, kext_scr[kv])
            m = jnp.max(s, axis=-1, keepdims=True)
            p = jnp.exp(s - m).astype(BF16)
            ov = _dot(p, vext_scr[kv, parity])
            halves.append(ov / pltpu.roll(ov, ATT_HEAD_DIM, axis=1))
        o = jnp.where(first, halves[0], halves[1])
        y = o * lax.rsqrt(_group_mean_sq(o, gmat) + EPS) * og_ref[...]
        y_ref[pair] = y.astype(y_ref.dtype)


def _attn_call(z4, cos, sin, q_gain, k_gain, out_gain, tq):
    _, nb, seq, _ = z4.shape
    nq = seq // tq
    reps = LANES // ATT_HEAD_DIM
    grp = (np.arange(LANES)[:, None] // ATT_HEAD_DIM) == (np.arange(LANES)[None, :] // ATT_HEAD_DIM)
    gmat = jnp.asarray(grp.astype(np.float32) / ATT_HEAD_DIM, BF16)

    def qspec(pair):
        return pl.BlockSpec((None, None, tq, LANES), lambda b, i: (CB_AQ + pair, b, i, 0))

    def full(cb):
        return pl.BlockSpec((None, None, seq, LANES), lambda b, i: (cb, b, 0, 0))

    vec = pl.BlockSpec((1, LANES), lambda b, i: (0, 0))
    return pl.pallas_call(
        _attn_kernel,
        grid=(nb, nq),
        in_specs=[
            qspec(0), qspec(1), qspec(2), qspec(3), full(CB_AK), full(CB_AV),
            pl.BlockSpec((tq, LANES), lambda b, i: (i, 0)),
            pl.BlockSpec((tq, LANES), lambda b, i: (i, 0)),
            pl.BlockSpec((seq, LANES), lambda b, i: (0, 0)),
            pl.BlockSpec((seq, LANES), lambda b, i: (0, 0)),
            vec, vec, vec,
            pl.BlockSpec((LANES, LANES), lambda b, i: (0, 0)),
        ],
        out_specs=pl.BlockSpec((ATT_PAIRS, tq, LANES), lambda b, i: (0, b * nq + i, 0)),
        out_shape=jax.ShapeDtypeStruct((ATT_PAIRS, nb * seq, LANES), BF16),
        scratch_shapes=[
            pltpu.VMEM((ATT_KV_HEADS, seq, LANES), BF16),
            pltpu.VMEM((ATT_KV_HEADS, 2, seq, LANES), BF16),
        ],
        compiler_params=pltpu.CompilerParams(
            dimension_semantics=("arbitrary", "arbitrary"), vmem_limit_bytes=VMEM_LIMIT_BYTES),
        name="attn",
    )(z4, z4, z4, z4, z4, z4, cos, sin, cos, sin,
      jnp.tile(q_gain, reps).reshape(1, LANES), jnp.tile(k_gain, reps).reshape(1, LANES),
      jnp.tile(out_gain, reps).reshape(1, LANES), gmat)


def _outffn_kernel(ah_ref, aa_ref, x_ref, mod_ref, g_post_mix_ref, g_pre_ff_ref, g_post_ff_ref,
                   wo_ref, w1_ref, w2_ref, o_ref):
    gate1 = mod_ref[2:3, :]
    shift2 = mod_ref[3:4, :]
    scale2 = mod_ref[4:5, :]
    gate2 = mod_ref[5:6, :]
    a = jnp.concatenate([ah_ref[j] for j in range(ah_ref.shape[0])]
                        + [aa_ref[j] for j in range(aa_ref.shape[0])], axis=1)
    y = _dot(a, wo_ref[...])
    x1 = x_ref[...] + gate1 * _row_rms(y, g_post_mix_ref[...])
    h = (_row_rms(x1, g_pre_ff_ref[...]) * (1.0 + scale2) + shift2).astype(BF16)
    d_ff = w1_ref.shape[1]
    fc = 1024
    acc = None
    for c0 in range(0, d_ff, fc):
        u = jnp.maximum(_dot(h, w1_ref[:, c0:c0 + fc]), 0.0)
        part = _dot((u * u).astype(BF16), w2_ref[c0:c0 + fc, :])
        acc = part if acc is None else acc + part
    o_ref[...] = x1 + gate2 * _row_rms(acc, g_post_ff_ref[...])


def _outffn_call(ah, aa, x2d, mod_l, g_post_mix, g_pre_ff, g_post_ff, wo_bf, w1_bf, w2_bf, seq, tm):
    rows, d = x2d.shape
    d_ff = w1_bf.shape[1]
    tiles_per_seq = seq // tm
    vec = pl.BlockSpec((1, d), lambda i: (0, 0))

    def resident(shape):
        return pl.BlockSpec(shape, lambda i: (0, 0), pipeline_mode=pl.Buffered(1))

    return pl.pallas_call(
        _outffn_kernel,
        grid=(rows // tm,),
        in_specs=[
            pl.BlockSpec((ah.shape[0], tm, LANES), lambda i: (0, i, 0)),
            pl.BlockSpec((aa.shape[0], tm, LANES), lambda i: (0, i, 0)),
            pl.BlockSpec((tm, d), lambda i: (i, 0)),
            pl.BlockSpec((None, N_MOD, d), lambda i: (i // tiles_per_seq, 0, 0)),
            vec, vec, vec,
            resident((d, d)), resident((d, d_ff)), resident((d_ff, d)),
        ],
        out_specs=pl.BlockSpec((tm, d), lambda i: (i, 0)),
        out_shape=jax.ShapeDtypeStruct((rows, d), F32),
        compiler_params=pltpu.CompilerParams(
            dimension_semantics=("arbitrary",), vmem_limit_bytes=VMEM_LIMIT_BYTES),
        name="outffn",
    )(ah, aa, x2d, mod_l, g_post_mix, g_pre_ff, g_post_ff, wo_bf, w1_bf, w2_bf)


def kernel(x, c, w_ada, ada_layer_bias, hg_lb_logits, pre_mix, post_mix, w_in, w_out, hg_out_gain,
           q_gain, k_gain, att_out_gain, pre_ff, post_ff, w_ff1, w_ff2):
    nb, seq, d = x.shape
    depth = w_in.shape[0]
    assert w_in.shape[2] == N_COL_BLOCKS * LANES and d == HG_WIDTH + ATT_WIDTH
    tm = min(512, seq)
    tq = min(256, seq)

    mod, lb_table = _cond_call(c, w_ada, ada_layer_bias, hg_lb_logits)
    mod = mod.reshape(depth, nb, N_MOD, d)
    cos, sin = _rope_tables(seq)

    x2d = x.reshape(nb * seq, d)
    for l in range(depth):
        z = _inproj_call(x2d, mod[l], pre_mix[l].reshape(1, d), w_in[l].astype(BF16), seq, tm)
        z4 = z.reshape(N_COL_BLOCKS, nb, seq, LANES)
        y_hgrn = _hgrn_call(z4, lb_table[l], hg_out_gain[l])
        y_att = _attn_call(z4, cos, sin, q_gain[l], k_gain[l], att_out_gain[l], tq)
        x2d = _outffn_call(y_hgrn, y_att, x2d, mod[l], post_mix[l].reshape(1, d),
                           pre_ff[l].reshape(1, d), post_ff[l].reshape(1, d),
                           w_out[l].astype(BF16), w_ff1[l].astype(BF16), w_ff2[l].astype(BF16), seq, tm)
    return x2d.reshape(nb, seq, d)
```

```python
import functools

import numpy as np
import jax
import jax.numpy as jnp
from jax import lax
from jax.experimental import pallas as pl
from jax.experimental.pallas import tpu as pltpu

F32 = jnp.float32
BF16 = jnp.bfloat16

LANES = 128
VMEM_LIMIT_BYTES = 56 * 1024 * 1024

EPS = 1e-6
N_MOD = 6
HG_HEADS = 4
HG_HEAD_DIM = 128
HG_WIDTH = HG_HEADS * HG_HEAD_DIM
HG_CHUNK = 64
HG_GROUP = 2
HG_LEVEL_HALVES = (32, 16, 8, 4, 2, 1)
ATT_HEADS = 8
ATT_HEAD_DIM = 64
ATT_KV_HEADS = 2
ATT_WIDTH = ATT_HEADS * ATT_HEAD_DIM
KV_WIDTH = ATT_KV_HEADS * ATT_HEAD_DIM
GRID_W = 64
ROPE_THETA = 10000.0

CB_HQ, CB_FFW, CB_FBW, CB_HI, CB_HG = 0, 4, 8, 12, 16
CB_AQ, CB_AK, CB_AV = 20, 24, 25
N_COL_BLOCKS = 26
ATT_PAIRS = ATT_WIDTH // LANES
PAIRS_PER_KV = ATT_PAIRS // ATT_KV_HEADS


def _dot(a, b):
    return jnp.dot(a, b, preferred_element_type=F32)


def _dot_nt(a, b):
    return lax.dot_general(a, b, (((1,), (1,)), ((), ())), preferred_element_type=F32)


def _split2(x):
    hi = x.astype(BF16)
    lo = (x - hi.astype(F32)).astype(BF16)
    return hi, lo


def _split3(x):
    hi = x.astype(BF16)
    r1 = x - hi.astype(F32)
    mid = r1.astype(BF16)
    lo = (r1 - mid.astype(F32)).astype(BF16)
    return hi, mid, lo


def _sigmoid(x):
    t = jnp.exp(-jnp.abs(x))
    return jnp.where(x >= 0, 1.0, t) / (1.0 + t)


def _row_rms(x, gain):
    ms = jnp.mean(x * x, axis=-1, keepdims=True)
    return x * lax.rsqrt(ms + EPS) * gain


def _group_mean_sq(x, gmat):
    hi, lo = _split2(x * x)
    return _dot(hi, gmat) + _dot(lo, gmat)


def _cond_kernel(c_ref, w_ref, bias_ref, lbl_ref, mod_ref, lb_ref):
    depth = bias_ref.shape[0]
    c = c_ref[...]
    s = c * _sigmoid(c)
    s_hi = s.astype(BF16).astype(F32)
    s_lo = s - s_hi
    w_hi, w_lo = _split2(w_ref[...])
    nb = c.shape[0]
    r1 = _dot(jnp.concatenate([s_hi, s_lo], axis=0).astype(BF16), w_hi)
    r2 = _dot(jnp.concatenate([s_hi, s_hi], axis=0).astype(BF16), w_lo)
    cond = r1[:nb] + r1[nb:] + r2[:nb]
    for l in range(depth):
        mod_ref[l] = cond + bias_ref[l:l + 1, :]

    rows = [lbl_ref[l:l + 1, :] for l in range(depth)]
    m = functools.reduce(jnp.maximum, rows)
    es = [jnp.exp(r - m) for r in rows]
    tot = functools.reduce(lambda a, b: a + b, es)
    cum = None
    first = None
    for l in range(depth):
        p = es[l] / tot
        cum = p if cum is None else cum + p
        if first is None:
            first = cum
        lb_ref[l:l + 1, :] = cum - first


def _cond_call(c, w_ada, ada_layer_bias, hg_lb_logits):
    nb, d = c.shape
    n_out = w_ada.shape[1]
    depth = ada_layer_bias.shape[0]
    tn = n_out // 4
    return pl.pallas_call(
        _cond_kernel,
        grid=(n_out // tn,),
        in_specs=[
            pl.BlockSpec((nb, d), lambda j: (0, 0)),
            pl.BlockSpec((d, tn), lambda j: (0, j)),
            pl.BlockSpec((depth, tn), lambda j: (0, j)),
            pl.BlockSpec(hg_lb_logits.shape, lambda j: (0, 0)),
        ],
        out_specs=[
            pl.BlockSpec((depth, nb, tn), lambda j: (0, 0, j)),
            pl.BlockSpec(hg_lb_logits.shape, lambda j: (0, 0)),
        ],
        out_shape=[
            jax.ShapeDtypeStruct((depth, nb, n_out), F32),
            jax.ShapeDtypeStruct(hg_lb_logits.shape, F32),
        ],
        compiler_params=pltpu.CompilerParams(
            dimension_semantics=("arbitrary",), vmem_limit_bytes=VMEM_LIMIT_BYTES),
        name="cond",
    )(c, w_ada, ada_layer_bias, hg_lb_logits)


def _inproj_kernel(x_ref, mod_ref, gain_ref, w_ref, z_ref):
    x = x_ref[...]
    shift = mod_ref[0:1, :]
    scale = mod_ref[1:2, :]
    h = _row_rms(x, gain_ref[...]) * (1.0 + scale) + shift
    hb = h.astype(BF16)
    n_blocks = z_ref.shape[0]
    for cb in range(0, n_blocks, 2):
        z = _dot(hb, w_ref[:, cb * LANES:(cb + 2) * LANES])
        z_ref[cb] = z[:, :LANES]
        z_ref[cb + 1] = z[:, LANES:]


def _inproj_call(x2d, mod_l, gain, w_in_bf, seq, tm):
    rows, d = x2d.shape
    n_cols = w_in_bf.shape[1]
    n_blocks = n_cols // LANES
    tiles_per_seq = seq // tm
    return pl.pallas_call(
        _inproj_kernel,
        grid=(rows // tm,),
        in_specs=[
            pl.BlockSpec((tm, d), lambda i: (i, 0)),
            pl.BlockSpec((None, N_MOD, d), lambda i: (i // tiles_per_seq, 0, 0)),
            pl.BlockSpec((1, d), lambda i: (0, 0)),
            pl.BlockSpec((d, n_cols), lambda i: (0, 0), pipeline_mode=pl.Buffered(1)),
        ],
        out_specs=pl.BlockSpec((n_blocks, tm, LANES), lambda i: (0, i, 0)),
        out_shape=jax.ShapeDtypeStruct((n_blocks, rows, LANES), F32),
        compiler_params=pltpu.CompilerParams(
            dimension_semantics=("arbitrary",), vmem_limit_bytes=VMEM_LIMIT_BYTES),
        name="inproj",
    )(x2d, mod_l, gain, w_in_bf)


def _hgrn_tables():
    c = HG_CHUNK
    t = np.arange(c)[:, None]
    s = np.arange(c)[None, :]
    tri = np.stack([(s <= t), (s >= t)]).astype(np.float32)
    fwd = []
    for half in HG_LEVEL_HALVES:
        same = (t // (2 * half)) == (s // (2 * half))
        t_low = (t // half) % 2 == 1
        s_up = (s // half) % 2 == 0
        fwd.append(same & t_low & s_up)
    fwd.append(t == s)
    fwd = np.stack(fwd).astype(np.float32)
    masks = np.stack([fwd, fwd.transpose(0, 2, 1)])
    assert HG_GROUP == 2
    return tri, np.tile(masks, (1, 1, 1, HG_GROUP))


def _hgrn_kernel(q_ref, ffw_ref, fbw_ref, i_ref, g_ref, lb_ref, gain_ref, tri_ref, mask_ref,
                 y_ref, b_scr, st_scr, o_scr, qs_scr):
    seq = q_ref.shape[0]
    n_chunks = seq // HG_CHUNK
    c = HG_CHUNK
    grp = HG_GROUP
    slab = 256
    lb = lb_ref[...]
    one_m_lb = 1.0 - lb
    row = lax.broadcasted_iota(jnp.int32, (c, LANES), 0)
    sub = lax.broadcasted_iota(jnp.int32, (8, LANES), 0)

    st_scr[...] = jnp.zeros(st_scr.shape, F32)

    def silu_q(i, carry):
        r0 = pl.multiple_of(i * slab, slab)
        zq = q_ref[pl.ds(r0, slab), :]
        qs_scr[pl.ds(r0, slab), :] = zq * _sigmoid(zq)
        return carry

    lax.fori_loop(0, seq // slab, silu_q, 0)

    def mid_reference(slot, b, half):
        def brow(r, n):
            return jnp.broadcast_to(b_scr[slot, r:r + 1, :], (n, LANES))

        if half == c // 2:
            return brow(half, c)
        if half >= 4:
            return jnp.concatenate(
                [brow(base + half, 2 * half) for base in range(0, c, 2 * half)], axis=0)
        if half == 2:
            return jnp.concatenate(
                [jnp.where(sub < 4, brow(base + 2, 8), brow(base + 6, 8))
                 for base in range(0, c, 8)], axis=0)
        return jnp.where((row & 1) == 0, pltpu.roll(b, c - 1, axis=0), b)

    def body(i, carry):
        insts = [(0, ffw_ref, pl.multiple_of((i * grp + j) * c, c)) for j in range(grp)]
        insts += [(1, fbw_ref, pl.multiple_of((n_chunks - 1 - (i * grp + j)) * c, c))
                  for j in range(grp)]
        pairs = [(d, [s for s, inst in enumerate(insts) if inst[0] == d]) for d in range(2)]

        qs_l, k_l, v_l, b_l = [], [], [], []
        for slot, (d, f_ref, r0) in enumerate(insts):
            zf = f_ref[pl.ds(r0, c), :]
            t = jnp.exp(-jnp.abs(zf))
            neg = zf < 0
            den = 1.0 + t
            num = jnp.where(neg, lb + t, lb * t + 1.0)
            g = jnp.where(num > 0, jnp.log(num), zf) - jnp.log(den)
            k = one_m_lb * (jnp.where(neg, 1.0, t) / den)
            g3 = jnp.concatenate(_split3(g), axis=1)
            b3 = _dot(tri_ref[d], g3)
            b = b3[:, :LANES] + b3[:, LANES:2 * LANES] + b3[:, 2 * LANES:]
            b_scr[slot] = b
            qs_l.append(qs_scr[pl.ds(r0, c), :])
            k_l.append(k)
            v_l.append(i_ref[pl.ds(r0, c), :])
            b_l.append(b)

        o_l = [None] * len(insts)
        for d, slots in pairs:
            st = st_scr[d]
            for slot in slots:
                edge = c - 1 if d == 0 else 0
                b_edge = b_scr[slot, edge:edge + 1, :]
                qe = (qs_l[slot] * jnp.exp(b_l[slot])).astype(BF16)
                kd = (k_l[slot] * jnp.exp(b_edge - b_l[slot])).astype(BF16)
                o_l[slot] = _dot_nt(qe, st.astype(BF16))
                st = st * jnp.exp(b_edge) + _dot(v_l[slot].T.astype(BF16), kd)
            st_scr[d] = st

        zero = jnp.zeros((c, LANES), BF16)

        def block_diag(xa, xb):
            return jnp.concatenate([jnp.concatenate([xa, zero], axis=1),
                                    jnp.concatenate([zero, xb], axis=1)], axis=0)

        def pair_scores(xa_q, xa_k, xb_q, xb_k):
            return _dot_nt(jnp.concatenate([xa_q, xb_q], axis=1), block_diag(xa_k, xb_k))

        n_lvl = len(HG_LEVEL_HALVES)
        sc_l = []
        for d, (sa, sb) in pairs:
            sc_l.append(pair_scores(qs_l[sa].astype(BF16), k_l[sa].astype(BF16),
                                    qs_l[sb].astype(BF16), k_l[sb].astype(BF16))
                        * mask_ref[d, n_lvl])
        for lvl, half in enumerate(HG_LEVEL_HALVES):
            for p, (d, slots) in enumerate(pairs):
                xs = []
                for slot in slots:
                    b = b_l[slot]
                    e = jnp.exp(-jnp.abs(b - mid_reference(slot, b, half)))
                    q_rows = ((row & half) != 0) if d == 0 else ((row & half) == 0)
                    xs.append((jnp.where(q_rows, qs_l[slot], k_l[slot]) * e).astype(BF16))
                sc_l[p] = sc_l[p] + pair_scores(xs[0], xs[0], xs[1], xs[1]) * mask_ref[d, lvl]

        for p, (d, (sa, sb)) in enumerate(pairs):
            o_intra = _dot(sc_l[p].astype(BF16),
                           block_diag(v_l[sa].astype(BF16), v_l[sb].astype(BF16)))
            o_scr[d, pl.ds(insts[sa][2], c), :] = o_l[sa] + o_intra[:, :LANES]
            o_scr[d, pl.ds(insts[sb][2], c), :] = o_l[sb] + o_intra[:, LANES:]
        return carry

    lax.fori_loop(0, n_chunks // grp, body, 0)

    gain = gain_ref[...]

    def fin(i, carry):
        r0 = pl.multiple_of(i * slab, slab)
        o = o_scr[0, pl.ds(r0, slab), :] + o_scr[1, pl.ds(r0, slab), :]
        zg = g_ref[pl.ds(r0, slab), :]
        y = _row_rms(o, gain) * (zg * _sigmoid(zg))
        y_ref[pl.ds(r0, slab), :] = y.astype(y_ref.dtype)
        return carry

    lax.fori_loop(0, seq // slab, fin, 0)


def _hgrn_call(z4, lb_l, out_gain):
    _, nb, seq, _ = z4.shape
    tri, masks = _hgrn_tables()

    def zspec(cb0):
        return pl.BlockSpec((None, None, seq, LANES), lambda b, h: (cb0 + h, b, 0, 0))

    y = pl.pallas_call(
        _hgrn_kernel,
        grid=(nb, HG_HEADS),
        in_specs=[
            zspec(CB_HQ), zspec(CB_FFW), zspec(CB_FBW), zspec(CB_HI), zspec(CB_HG),
            pl.BlockSpec((None, 1, LANES), lambda b, h: (h, 0, 0)),
            pl.BlockSpec((1, LANES), lambda b, h: (0, 0)),
            pl.BlockSpec(tri.shape, lambda b, h: (0, 0, 0)),
            pl.BlockSpec(masks.shape, lambda b, h: (0, 0, 0, 0)),
        ],
        out_specs=pl.BlockSpec((None, None, seq, LANES), lambda b, h: (h, b, 0, 0)),
        out_shape=jax.ShapeDtypeStruct((HG_HEADS, nb, seq, LANES), BF16),
        scratch_shapes=[
            pltpu.VMEM((2 * HG_GROUP, HG_CHUNK, LANES), F32),
            pltpu.VMEM((2, HG_HEAD_DIM, HG_HEAD_DIM), F32),
            pltpu.VMEM((2, seq, LANES), F32),
            pltpu.VMEM((seq, LANES), F32),
        ],
        compiler_params=pltpu.CompilerParams(
            dimension_semantics=("arbitrary", "arbitrary"), vmem_limit_bytes=VMEM_LIMIT_BYTES),
        name="hgrn",
    )(z4, z4, z4, z4, z4, lb_l.reshape(HG_HEADS, 1, LANES), out_gain.reshape(1, LANES),
      jnp.asarray(tri, BF16), jnp.asarray(masks, F32))
    return y.reshape(HG_HEADS, nb * seq, LANES)


def _rope_tables(seq):
    pos = jnp.arange(seq)
    row = (pos // GRID_W).astype(F32)
    col = (pos % GRID_W).astype(F32)
    half = ATT_HEAD_DIM // 2
    inv_freq = 1.0 / (ROPE_THETA ** (jnp.arange(0, half, 2, dtype=F32) / half))
    ang_r = row[:, None] * inv_freq
    ang_c = col[:, None] * inv_freq
    cos = jnp.concatenate([jnp.cos(ang_r)] * 2 + [jnp.cos(ang_c)] * 2, axis=-1)
    sin = jnp.concatenate([-jnp.sin(ang_r), jnp.sin(ang_r), -jnp.sin(ang_c), jnp.sin(ang_c)], axis=-1)
    reps = LANES // ATT_HEAD_DIM
    return jnp.tile(cos, (1, reps)), jnp.tile(sin, (1, reps))


def _rope(x, cos, sin, lane):
    quarter = ATT_HEAD_DIM // 4
    fwd = pltpu.roll(x, LANES - quarter, axis=1)
    bwd = pltpu.roll(x, quarter, axis=1)
    partner = jnp.where((lane & quarter) == 0, fwd, bwd)
    return x * cos + partner * sin


def _attn_kernel(q0_ref, q1_ref, q2_ref, q3_ref, k_ref, v_ref, cosq_ref, sinq_ref, cosk_ref, sink_ref,
                 qg_ref, kg_ref, og_ref, gmat_ref, y_ref, kext_scr, vext_scr):
    seq = k_ref.shape[0]
    tq = q0_ref.shape[0]
    gmat = gmat_ref[...]

    @pl.when(pl.program_id(1) == 0)
    def _prepare_kv():
        slab = min(512, seq)
        lane = lax.broadcasted_iota(jnp.int32, (slab, LANES), 1)
        first = lane < ATT_HEAD_DIM
        for r0 in range(0, seq, slab):
            rs = slice(r0, r0 + slab)
            k = k_ref[rs, :]
            kn = k * lax.rsqrt(_group_mean_sq(k, gmat) + EPS) * kg_ref[...]
            kr = _rope(kn, cosk_ref[rs, :], sink_ref[rs, :], lane)
            kr_sw = pltpu.roll(kr, ATT_HEAD_DIM, axis=1)
            kext_scr[0, rs, :] = jnp.where(first, kr, kr_sw).astype(BF16)
            kext_scr[1, rs, :] = jnp.where(first, kr_sw, kr).astype(BF16)
            v = v_ref[rs, :]
            v_sw = pltpu.roll(v, ATT_HEAD_DIM, axis=1)
            vext_scr[0, 0, rs, :] = jnp.where(first, v, 1.0).astype(BF16)
            vext_scr[0, 1, rs, :] = jnp.where(first, 1.0, v_sw).astype(BF16)
            vext_scr[1, 0, rs, :] = jnp.where(first, v_sw, 1.0).astype(BF16)
            vext_scr[1, 1, rs, :] = jnp.where(first, 1.0, v).astype(BF16)

    lane = lax.broadcasted_iota(jnp.int32, (tq, LANES), 1)
    first = lane < ATT_HEAD_DIM
    cosq = cosq_ref[...]
    sinq = sinq_ref[...]
    sm_scale = ATT_HEAD_DIM ** -0.5
    for pair, q_ref in enumerate((q0_ref, q1_ref, q2_ref, q3_ref)):
        kv = pair // PAIRS_PER_KV
        q = q_ref[...]
        qn = q * lax.rsqrt(_group_mean_sq(q, gmat) + EPS) * (qg_ref[...] * sm_scale)
        qr = _rope(qn, cosq, sinq, lane)
        halves = []
        for parity in range(2):
            keep = first if parity == 0 else jnp.logical_not(first)
            qm = jnp.where(keep, qr, 0.0).astype(BF16)
            s = _dot_nt(qm, kext_scr[kv])
            m = jnp.max(s, axis=-1, keepdims=True)
            p = jnp.exp(s - m).astype(BF16)
            ov = _dot(p, vext_scr[kv, parity])
            halves.append(ov / pltpu.roll(ov, ATT_HEAD_DIM, axis=1))
        o = jnp.where(first, halves[0], halves[1])
        y = o * lax.rsqrt(_group_mean_sq(o, gmat) + EPS) * og_ref[...]
        y_ref[pair] = y.astype(y_ref.dtype)


def _attn_call(z4, cos, sin, q_gain, k_gain, out_gain, tq):
    _, nb, seq, _ = z4.shape
    nq = seq // tq
    reps = LANES // ATT_HEAD_DIM
    grp = (np.arange(LANES)[:, None] // ATT_HEAD_DIM) == (np.arange(LANES)[None, :] // ATT_HEAD_DIM)
    gmat = jnp.asarray(grp.astype(np.float32) / ATT_HEAD_DIM, BF16)

    def qspec(pair):
        return pl.BlockSpec((None, None, tq, LANES), lambda b, i: (CB_AQ + pair, b, i, 0))

    def full(cb):
        return pl.BlockSpec((None, None, seq, LANES), lambda b, i: (cb, b, 0, 0))

    vec = pl.BlockSpec((1, LANES), lambda b, i: (0, 0))
    return pl.pallas_call(
        _attn_kernel,
        grid=(nb, nq),
        in_specs=[
            qspec(0), qspec(1), qspec(2), qspec(3), full(CB_AK), full(CB_AV),
            pl.BlockSpec((tq, LANES), lambda b, i: (i, 0)),
            pl.BlockSpec((tq, LANES), lambda b, i: (i, 0)),
            pl.BlockSpec((seq, LANES), lambda b, i: (0, 0)),
            pl.BlockSpec((seq, LANES), lambda b, i: (0, 0)),
            vec, vec, vec,
            pl.BlockSpec((LANES, LANES), lambda b, i: (0, 0)),
        ],
        out_specs=pl.BlockSpec((ATT_PAIRS, tq, LANES), lambda b, i: (0, b * nq + i, 0)),
        out_shape=jax.ShapeDtypeStruct((ATT_PAIRS, nb * seq, LANES), BF16),
        scratch_shapes=[
            pltpu.VMEM((ATT_KV_HEADS, seq, LANES), BF16),
            pltpu.VMEM((ATT_KV_HEADS, 2, seq, LANES), BF16),
        ],
        compiler_params=pltpu.CompilerParams(
            dimension_semantics=("arbitrary", "arbitrary"), vmem_limit_bytes=VMEM_LIMIT_BYTES),
        name="attn",
    )(z4, z4, z4, z4, z4, z4, cos, sin, cos, sin,
      jnp.tile(q_gain, reps).reshape(1, LANES), jnp.tile(k_gain, reps).reshape(1, LANES),
      jnp.tile(out_gain, reps).reshape(1, LANES), gmat)


def _outffn_kernel(ah_ref, aa_ref, x_ref, mod_ref, g_post_mix_ref, g_pre_ff_ref, g_post_ff_ref,
                   wo_ref, w1_ref, w2_ref, o_ref):
    gate1 = mod_ref[2:3, :]
    shift2 = mod_ref[3:4, :]
    scale2 = mod_ref[4:5, :]
    gate2 = mod_ref[5:6, :]
    a = jnp.concatenate([ah_ref[j] for j in range(ah_ref.shape[0])]
                        + [aa_ref[j] for j in range(aa_ref.shape[0])], axis=1)
    y = _dot(a, wo_ref[...])
    x1 = x_ref[...] + gate1 * _row_rms(y, g_post_mix_ref[...])
    h = (_row_rms(x1, g_pre_ff_ref[...]) * (1.0 + scale2) + shift2).astype(BF16)
    d_ff = w1_ref.shape[1]
    fc = 1024
    acc = None
    for c0 in range(0, d_ff, fc):
        u = jnp.maximum(_dot(h, w1_ref[:, c0:c0 + fc]), 0.0)
        part = _dot((u * u).astype(BF16), w2_ref[c0:c0 + fc, :])
        acc = part if acc is None else acc + part
    o_ref[...] = x1 + gate2 * _row_rms(acc, g_post_ff_ref[...])


def _outffn_call(ah, aa, x2d, mod_l, g_post_mix, g_pre_ff, g_post_ff, wo_bf, w1_bf, w2_bf, seq, tm):
    rows, d = x2d.shape
    d_ff = w1_bf.shape[1]
    tiles_per_seq = seq // tm
    vec = pl.BlockSpec((1, d), lambda i: (0, 0))

    def resident(shape):
        return pl.BlockSpec(shape, lambda i: (0, 0), pipeline_mode=pl.Buffered(1))

    return pl.pallas_call(
        _outffn_kernel,
        grid=(rows // tm,),
        in_specs=[
            pl.BlockSpec((ah.shape[0], tm, LANES), lambda i: (0, i, 0)),
            pl.BlockSpec((aa.shape[0], tm, LANES), lambda i: (0, i, 0)),
            pl.BlockSpec((tm, d), lambda i: (i, 0)),
            pl.BlockSpec((None, N_MOD, d), lambda i: (i // tiles_per_seq, 0, 0)),
            vec, vec, vec,
            resident((d, d)), resident((d, d_ff)), resident((d_ff, d)),
        ],
        out_specs=pl.BlockSpec((tm, d), lambda i: (i, 0)),
        out_shape=jax.ShapeDtypeStruct((rows, d), F32),
        compiler_params=pltpu.CompilerParams(
            dimension_semantics=("arbitrary",), vmem_limit_bytes=VMEM_LIMIT_BYTES),
        name="outffn",
    )(ah, aa, x2d, mod_l, g_post_mix, g_pre_ff, g_post_ff, wo_bf, w1_bf, w2_bf)


def kernel(x, c, w_ada, ada_layer_bias, hg_lb_logits, pre_mix, post_mix, w_in, w_out, hg_out_gain,
           q_gain, k_gain, att_out_gain, pre_ff, post_ff, w_ff1, w_ff2):
    nb, seq, d = x.shape
    depth = w_in.shape[0]
    assert w_in.shape[2] == N_COL_BLOCKS * LANES and d == HG_WIDTH + ATT_WIDTH
    tm = min(512, seq)
    tq = min(256, seq)

    mod, lb_table = _cond_call(c, w_ada, ada_layer_bias, hg_lb_logits)
    mod = mod.reshape(depth, nb, N_MOD, d)
    cos, sin = _rope_tables(seq)

    x2d = x.reshape(nb * seq, d)
    for l in range(depth):
        z = _inproj_call(x2d, mod[l], pre_mix[l].reshape(1, d), w_in[l].astype(BF16), seq, tm)
        z4 = z.reshape(N_COL_BLOCKS, nb, seq, LANES)
        y_hgrn = _hgrn_call(z4, lb_table[l], hg_out_gain[l])
        y_att = _attn_call(z4, cos, sin, q_gain[l], k_gain[l], att_out_gain[l], tq)
        x2d = _outffn_call(y_hgrn, y_att, x2d, mod[l], post_mix[l].reshape(1, d),
                           pre_ff[l].reshape(1, d), post_ff[l].reshape(1, d),
                           w_out[l].astype(BF16), w_ff1[l].astype(BF16), w_ff2[l].astype(BF16), seq, tm)
    return x2d.reshape(nb, seq, d)
```

```python
import functools

import numpy as np
import jax
import jax.numpy as jnp
from jax import lax
from jax.experimental import pallas as pl
from jax.experimental.pallas import tpu as pltpu

F32 = jnp.float32
BF16 = jnp.bfloat16

LANES = 128
VMEM_LIMIT_BYTES = 56 * 1024 * 1024

EPS = 1e-6
N_MOD = 6
HG_HEADS = 4
HG_HEAD_DIM = 128
HG_WIDTH = HG_HEADS * HG_HEAD_DIM
HG_CHUNK = 64
HG_GROUP = 4
HG_PREP_ROWS = 256
LOG2E = 1.4426950408889634
HG_LEVEL_HALVES = (32, 16, 8, 4, 2, 1)
ATT_HEADS = 8
ATT_HEAD_DIM = 64
ATT_KV_HEADS = 2
ATT_WIDTH = ATT_HEADS * ATT_HEAD_DIM
KV_WIDTH = ATT_KV_HEADS * ATT_HEAD_DIM
GRID_W = 64
ROPE_THETA = 10000.0

CB_HQ, CB_FFW, CB_FBW, CB_HI, CB_HG = 0, 4, 8, 12, 16
CB_AQ, CB_AK, CB_AV = 20, 24, 25
N_COL_BLOCKS = 26
ATT_PAIRS = ATT_WIDTH // LANES
PAIRS_PER_KV = ATT_PAIRS // ATT_KV_HEADS


def _dot(a, b):
    return jnp.dot(a, b, preferred_element_type=F32)


def _dot_nt(a, b):
    return lax.dot_general(a, b, (((1,), (1,)), ((), ())), preferred_element_type=F32)


def _split2(x):
    hi = x.astype(BF16)
    lo = (x - hi.astype(F32)).astype(BF16)
    return hi, lo


def _split3(x):
    hi = x.astype(BF16)
    r1 = x - hi.astype(F32)
    mid = r1.astype(BF16)
    lo = (r1 - mid.astype(F32)).astype(BF16)
    return hi, mid, lo


def _sigmoid(x):
    t = jnp.exp(-jnp.abs(x))
    return jnp.where(x >= 0, 1.0, t) / (1.0 + t)


def _row_rms(x, gain):
    ms = jnp.mean(x * x, axis=-1, keepdims=True)
    return x * lax.rsqrt(ms + EPS) * gain


def _group_mean_sq(x, gmat):
    w = x.shape[1]
    r = _dot(jnp.concatenate(_split2(x * x), axis=1), gmat)
    return r[:, :w] + r[:, w:]


def _cond_kernel(c_ref, w_ref, bias_ref, lbl_ref, mod_ref, lb_ref):
    depth = bias_ref.shape[0]
    c = c_ref[...]
    s = c * _sigmoid(c)
    s_hi = s.astype(BF16).astype(F32)
    s_lo = s - s_hi
    w_hi, w_lo = _split2(w_ref[...])
    nb = c.shape[0]
    r1 = _dot(jnp.concatenate([s_hi, s_lo], axis=0).astype(BF16), w_hi)
    r2 = _dot(jnp.concatenate([s_hi, s_hi], axis=0).astype(BF16), w_lo)
    cond = r1[:nb] + r1[nb:] + r2[:nb]
    for l in range(depth):
        mod_ref[l] = cond + bias_ref[l:l + 1, :]

    rows = [lbl_ref[l:l + 1, :] for l in range(depth)]
    m = functools.reduce(jnp.maximum, rows)
    es = [jnp.exp(r - m) for r in rows]
    tot = functools.reduce(lambda a, b: a + b, es)
    cum = None
    first = None
    for l in range(depth):
        p = es[l] / tot
        cum = p if cum is None else cum + p
        if first is None:
            first = cum
        lb_ref[l:l + 1, :] = cum - first


def _cond_call(c, w_ada, ada_layer_bias, hg_lb_logits):
    nb, d = c.shape
    n_out = w_ada.shape[1]
    depth = ada_layer_bias.shape[0]
    tn = n_out // 4
    return pl.pallas_call(
        _cond_kernel,
        grid=(n_out // tn,),
        in_specs=[
            pl.BlockSpec((nb, d), lambda j: (0, 0)),
            pl.BlockSpec((d, tn), lambda j: (0, j)),
            pl.BlockSpec((depth, tn), lambda j: (0, j)),
            pl.BlockSpec(hg_lb_logits.shape, lambda j: (0, 0)),
        ],
        out_specs=[
            pl.BlockSpec((depth, nb, tn), lambda j: (0, 0, j)),
            pl.BlockSpec(hg_lb_logits.shape, lambda j: (0, 0)),
        ],
        out_shape=[
            jax.ShapeDtypeStruct((depth, nb, n_out), F32),
            jax.ShapeDtypeStruct(hg_lb_logits.shape, F32),
        ],
        compiler_params=pltpu.CompilerParams(
            dimension_semantics=("arbitrary",), vmem_limit_bytes=VMEM_LIMIT_BYTES),
        name="cond",
    )(c, w_ada, ada_layer_bias, hg_lb_logits)


def _inproj_kernel(x_ref, mod_ref, gain_ref, w_ref, z_ref):
    x = x_ref[...]
    shift = mod_ref[0:1, :]
    scale = mod_ref[1:2, :]
    h = _row_rms(x, gain_ref[...]) * (1.0 + scale) + shift
    hb = h.astype(BF16)
    n_blocks = z_ref.shape[0]
    for cb in range(0, n_blocks, 2):
        z = _dot(hb, w_ref[:, cb * LANES:(cb + 2) * LANES])
        z_ref[cb] = z[:, :LANES]
        z_ref[cb + 1] = z[:, LANES:]


def _inproj_call(x2d, mod_l, gain, w_in_bf, seq, tm):
    rows, d = x2d.shape
    n_cols = w_in_bf.shape[1]
    n_blocks = n_cols // LANES
    tiles_per_seq = seq // tm
    return pl.pallas_call(
        _inproj_kernel,
        grid=(rows // tm,),
        in_specs=[
            pl.BlockSpec((tm, d), lambda i: (i, 0)),
            pl.BlockSpec((None, N_MOD, d), lambda i: (i // tiles_per_seq, 0, 0)),
            pl.BlockSpec((1, d), lambda i: (0, 0)),
            pl.BlockSpec((d, n_cols), lambda i: (0, 0), pipeline_mode=pl.Buffered(1)),
        ],
        out_specs=pl.BlockSpec((n_blocks, tm, LANES), lambda i: (0, i, 0)),
        out_shape=jax.ShapeDtypeStruct((n_blocks, rows, LANES), F32),
        compiler_params=pltpu.CompilerParams(
            dimension_semantics=("arbitrary",), vmem_limit_bytes=VMEM_LIMIT_BYTES),
        name="inproj",
    )(x2d, mod_l, gain, w_in_bf)


def _hgrn_tables():
    c = HG_CHUNK
    n = HG_PREP_ROWS
    tt = np.arange(n)[:, None]
    ss = np.arange(n)[None, :]
    same_chunk = (tt // c) == (ss // c)
    tri = np.stack([same_chunk & (ss <= tt), same_chunk & (ss >= tt)]).astype(np.float32)
    t = np.arange(c)[:, None]
    s = np.arange(c)[None, :]
    fwd = []
    for half in HG_LEVEL_HALVES:
        same = (t // (2 * half)) == (s // (2 * half))
        t_low = (t // half) % 2 == 1
        s_up = (s // half) % 2 == 0
        fwd.append(same & t_low & s_up)
    fwd.append(t == s)
    fwd = np.stack(fwd).astype(np.float32)
    masks = np.stack([fwd, fwd.transpose(0, 2, 1)])
    return tri, np.tile(masks, (1, 1, 1, 2))


def _hgrn_kernel(q_ref, ffw_ref, fbw_ref, i_ref, g_ref, lb_ref, gain_ref, tri_ref, mask_ref,
                 y_ref, st_scr, o_scr, qs_scr, k_scr, b_scr):
    seq = q_ref.shape[0]
    n_chunks = seq // HG_CHUNK
    c = HG_CHUNK
    grp = HG_GROUP
    slab = min(HG_PREP_ROWS, seq)
    n_sub = 2 if seq % (2 * slab) == 0 else 1
    lb = lb_ref[...]
    one_m_lb = 1.0 - lb
    row = lax.broadcasted_iota(jnp.int32, (c, LANES), 0)
    sub = lax.broadcasted_iota(jnp.int32, (8, LANES), 0)

    st_scr[...] = jnp.zeros(st_scr.shape, F32)

    def prep(i, carry):
        work = []
        for j in range(n_sub):
            rows = pl.ds(pl.multiple_of((i * n_sub + j) * slab, slab), slab)
            zq = q_ref[rows, :]
            qs_scr[rows, :] = zq * _sigmoid(zq)
            for d, f_ref in enumerate((ffw_ref, fbw_ref)):
                zf = f_ref[rows, :]
                t = jnp.exp(-jnp.abs(zf))
                neg = zf < 0
                inv = 1.0 / (1.0 + t)
                num = jnp.where(neg, lb + t, lb * t + 1.0)
                g = jnp.where(num > 0, jnp.log(num * inv), zf) * LOG2E
                k_scr[d, rows, :] = one_m_lb * (jnp.where(neg, 1.0, t) * inv)
                work.append((d, rows, jnp.concatenate(_split3(g), axis=1)))
        sums = [_dot(tri_ref[d, :slab, :slab], g3) for d, _, g3 in work]
        for (d, rows, _), b3 in zip(work, sums):
            b_scr[d, rows, :] = b3[:, :LANES] + b3[:, LANES:2 * LANES] + b3[:, 2 * LANES:]
        return carry

    lax.fori_loop(0, seq // (slab * n_sub), prep, 0)

    def mid_reference(d, r0, b, half):
        def brow(r, n):
            return jnp.broadcast_to(b_scr[d, pl.ds(r0 + r, 1), :], (n, LANES))

        if half == c // 2:
            return brow(half, c)
        if half >= 4:
            return jnp.concatenate(
                [brow(base + half, 2 * half) for base in range(0, c, 2 * half)], axis=0)
        if half == 2:
            return jnp.concatenate(
                [jnp.where(sub < 4, brow(base + 2, 8), brow(base + 6, 8))
                 for base in range(0, c, 8)], axis=0)
        return jnp.where((row & 1) == 0, pltpu.roll(b, c - 1, axis=0), b)

    def body(i, carry):
        insts = [(0, pl.multiple_of((i * grp + j) * c, c)) for j in range(grp)]
        insts += [(1, pl.multiple_of((n_chunks - 1 - (i * grp + j)) * c, c)) for j in range(grp)]
        pairs = [(d, (s, s + 1)) for d in range(2) for s in range(d * grp, (d + 1) * grp, 2)]

        qs_l = [qs_scr[pl.ds(r0, c), :] for _, r0 in insts]
        k_l = [k_scr[d, pl.ds(r0, c), :] for d, r0 in insts]
        v_l = [i_ref[pl.ds(r0, c), :] for _, r0 in insts]
        b_l = [b_scr[d, pl.ds(r0, c), :] for d, r0 in insts]

        o_l = [None] * len(insts)
        for d in range(2):
            st = st_scr[d]
            for slot in range(d * grp, (d + 1) * grp):
                edge = c - 1 if d == 0 else 0
                b_edge = b_scr[d, pl.ds(insts[slot][1] + edge, 1), :]
                qe = (qs_l[slot] * jnp.exp2(b_l[slot])).astype(BF16)
                kd = (k_l[slot] * jnp.exp2(b_edge - b_l[slot])).astype(BF16)
                o_l[slot] = _dot_nt(qe, st.astype(BF16))
                st = st * jnp.exp2(b_edge) + _dot(v_l[slot].T.astype(BF16), kd)
            st_scr[d] = st

        zero = jnp.zeros((c, LANES), BF16)

        def block_diag(xa, xb):
            return jnp.concatenate([jnp.concatenate([xa, zero], axis=1),
                                    jnp.concatenate([zero, xb], axis=1)], axis=0)

        def pair_scores(xa_q, xa_k, xb_q, xb_k):
            return _dot_nt(jnp.concatenate([xa_q, xb_q], axis=1), block_diag(xa_k, xb_k))

        n_lvl = len(HG_LEVEL_HALVES)
        sc_l = []
        for d, (sa, sb) in pairs:
            sc_l.append(pair_scores(qs_l[sa].astype(BF16), k_l[sa].astype(BF16),
                                    qs_l[sb].astype(BF16), k_l[sb].astype(BF16))
                        * mask_ref[d, n_lvl])
        for lvl, half in enumerate(HG_LEVEL_HALVES):
            for p, (d, slots) in enumerate(pairs):
                xs = []
                for slot in slots:
                    b = b_l[slot]
                    q_rows = ((row & half) != 0) if d == 0 else ((row & half) == 0)
                    sign = jnp.where(q_rows, 1.0, -1.0)
                    e = jnp.exp2((b - mid_reference(d, insts[slot][1], b, half)) * sign)
                    xs.append((jnp.where(q_rows, qs_l[slot], k_l[slot]) * e).astype(BF16))
                sc_l[p] = sc_l[p] + pair_scores(xs[0], xs[0], xs[1], xs[1]) * mask_ref[d, lvl]

        for p, (d, (sa, sb)) in enumerate(pairs):
            o_intra = _dot(sc_l[p].astype(BF16),
                           block_diag(v_l[sa].astype(BF16), v_l[sb].astype(BF16)))
            o_scr[d, pl.ds(insts[sa][1], c), :] = o_l[sa] + o_intra[:, :LANES]
            o_scr[d, pl.ds(insts[sb][1], c), :] = o_l[sb] + o_intra[:, LANES:]
        return carry

    lax.fori_loop(0, n_chunks // grp, body, 0)

    gain = gain_ref[...]

    def fin(i, carry):
        r0 = pl.multiple_of(i * slab, slab)
        o = o_scr[0, pl.ds(r0, slab), :] + o_scr[1, pl.ds(r0, slab), :]
        zg = g_ref[pl.ds(r0, slab), :]
        y = _row_rms(o, gain) * (zg * _sigmoid(zg))
        y_ref[pl.ds(r0, slab), :] = y.astype(y_ref.dtype)
        return carry

    lax.fori_loop(0, seq // slab, fin, 0)


def _hgrn_call(z4, lb_l, out_gain):
    _, nb, seq, _ = z4.shape
    tri, masks = _hgrn_tables()

    def zspec(cb0):
        return pl.BlockSpec((None, None, seq, LANES), lambda b, h: (cb0 + h, b, 0, 0))

    y = pl.pallas_call(
        _hgrn_kernel,
        grid=(nb, HG_HEADS),
        in_specs=[
            zspec(CB_HQ), zspec(CB_FFW), zspec(CB_FBW), zspec(CB_HI), zspec(CB_HG),
            pl.BlockSpec((None, 1, LANES), lambda b, h: (h, 0, 0)),
            pl.BlockSpec((1, LANES), lambda b, h: (0, 0)),
            pl.BlockSpec(tri.shape, lambda b, h: (0, 0, 0)),
            pl.BlockSpec(masks.shape, lambda b, h: (0, 0, 0, 0)),
        ],
        out_specs=pl.BlockSpec((None, None, seq, LANES), lambda b, h: (h, b, 0, 0)),
        out_shape=jax.ShapeDtypeStruct((HG_HEADS, nb, seq, LANES), BF16),
        scratch_shapes=[
            pltpu.VMEM((2, HG_HEAD_DIM, HG_HEAD_DIM), F32),
            pltpu.VMEM((2, seq, LANES), F32),
            pltpu.VMEM((seq, LANES), F32),
            pltpu.VMEM((2, seq, LANES), F32),
            pltpu.VMEM((2, seq, LANES), F32),
        ],
        compiler_params=pltpu.CompilerParams(
            dimension_semantics=("arbitrary", "arbitrary"), vmem_limit_bytes=VMEM_LIMIT_BYTES),
        name="hgrn",
    )(z4, z4, z4, z4, z4, lb_l.reshape(HG_HEADS, 1, LANES), out_gain.reshape(1, LANES),
      jnp.asarray(tri, BF16), jnp.asarray(masks, F32))
    return y.reshape(HG_HEADS, nb * seq, LANES)


def _rope_tables(seq):
    pos = jnp.arange(seq)
    row = (pos // GRID_W).astype(F32)
    col = (pos % GRID_W).astype(F32)
    half = ATT_HEAD_DIM // 2
    inv_freq = 1.0 / (ROPE_THETA ** (jnp.arange(0, half, 2, dtype=F32) / half))
    ang_r = row[:, None] * inv_freq
    ang_c = col[:, None] * inv_freq
    cos = jnp.concatenate([jnp.cos(ang_r)] * 2 + [jnp.cos(ang_c)] * 2, axis=-1)
    sin = jnp.concatenate([-jnp.sin(ang_r), jnp.sin(ang_r), -jnp.sin(ang_c), jnp.sin(ang_c)], axis=-1)
    reps = LANES // ATT_HEAD_DIM
    return jnp.tile(cos, (1, reps)), jnp.tile(sin, (1, reps))


def _rope(x, cos, sin, lane):
    quarter = ATT_HEAD_DIM // 4
    fwd = pltpu.roll(x, LANES - quarter, axis=1)
    bwd = pltpu.roll(x, quarter, axis=1)
    partner = jnp.where((lane & quarter) == 0, fwd, bwd)
    return x * cos + partner * sin


def _attn_kernel(q0_ref, q1_ref, q2_ref, q3_ref, k_ref, v_ref, cosq_ref, sinq_ref, cosk_ref, sink_ref,
                 qg_ref, kg_ref, og_ref, gmat_ref, y_ref, kext_scr, vext_scr):
    seq = k_ref.shape[0]
    tq = q0_ref.shape[0]
    gmat = gmat_ref[...]

    @pl.when(pl.program_id(1) == 0)
    def _prepare_kv():
        slab = min(512, seq)
        lane = lax.broadcasted_iota(jnp.int32, (slab, LANES), 1)
        first = lane < ATT_HEAD_DIM
        for r0 in range(0, seq, slab):
            rs = slice(r0, r0 + slab)
            k = k_ref[rs, :]
            kn = k * lax.rsqrt(_group_mean_sq(k, gmat) + EPS) * kg_ref[...]
            kr = _rope(kn, cosk_ref[rs, :], sink_ref[rs, :], lane)
            kr_sw = pltpu.roll(kr, ATT_HEAD_DIM, axis=1)
            kext_scr[0, rs, :] = jnp.where(first, kr, kr_sw).astype(BF16)
            kext_scr[1, rs, :] = jnp.where(first, kr_sw, kr).astype(BF16)
            v = v_ref[rs, :]
            v_sw = pltpu.roll(v, ATT_HEAD_DIM, axis=1)
            vext_scr[0, 0, rs, :] = jnp.where(first, v, 1.0).astype(BF16)
            vext_scr[0, 1, rs, :] = jnp.where(first, 1.0, v_sw).astype(BF16)
            vext_scr[1, 0, rs, :] = jnp.where(first, v_sw, 1.0).astype(BF16)
            vext_scr[1, 1, rs, :] = jnp.where(first, 1.0, v).astype(BF16)

    lane = lax.broadcasted_iota(jnp.int32, (tq, LANES), 1)
    first = lane < ATT_HEAD_DIM
    cosq = cosq_ref[...]
    sinq = sinq_ref[...]
    sm_scale = ATT_HEAD_DIM ** -0.5 * LOG2E
    for pair, q_ref in enumerate((q0_ref, q1_ref, q2_ref, q3_ref)):
        kv = pair // PAIRS_PER_KV
        q = q_ref[...]
        qn = q * lax.rsqrt(_group_mean_sq(q, gmat) + EPS) * (qg_ref[...] * sm_scale)
        qr = _rope(qn, cosq, sinq, lane)
        halves = []
        for parity in range(2):
            keep = first if parity == 0 else jnp.logical_not(first)
            qm = jnp.where(keep, qr, 0.0).astype(BF16)
            s = _dot_nt(qm, kext_scr[kv])
            m = jnp.max(s, axis=-1, keepdims=True)
            p = jnp.exp2(s - m).astype(BF16)
            ov = _dot(p, vext_scr[kv, parity])
            halves.append(ov / pltpu.roll(ov, ATT_HEAD_DIM, axis=1))
        o = jnp.where(first, halves[0], halves[1])
        y = o * lax.rsqrt(_group_mean_sq(o, gmat) + EPS) * og_ref[...]
        y_ref[pair] = y.astype(y_ref.dtype)


def _attn_call(z4, cos, sin, q_gain, k_gain, out_gain, tq):
    _, nb, seq, _ = z4.shape
    nq = seq // tq
    reps = LANES // ATT_HEAD_DIM
    lanes2 = np.arange(2 * LANES)
    grp = (lanes2[:, None] // ATT_HEAD_DIM) == (lanes2[None, :] // ATT_HEAD_DIM)
    gmat = jnp.asarray(grp.astype(np.float32) / ATT_HEAD_DIM, BF16)

    def qspec(pair):
        return pl.BlockSpec((None, None, tq, LANES), lambda b, i: (CB_AQ + pair, b, i, 0))

    def full(cb):
        return pl.BlockSpec((None, None, seq, LANES), lambda b, i: (cb, b, 0, 0))

    vec = pl.BlockSpec((1, LANES), lambda b, i: (0, 0))
    return pl.pallas_call(
        _attn_kernel,
        grid=(nb, nq),
        in_specs=[
            qspec(0), qspec(1), qspec(2), qspec(3), full(CB_AK), full(CB_AV),
            pl.BlockSpec((tq, LANES), lambda b, i: (i, 0)),
            pl.BlockSpec((tq, LANES), lambda b, i: (i, 0)),
            pl.BlockSpec((seq, LANES), lambda b, i: (0, 0)),
            pl.BlockSpec((seq, LANES), lambda b, i: (0, 0)),
            vec, vec, vec,
            pl.BlockSpec((2 * LANES, 2 * LANES), lambda b, i: (0, 0)),
        ],
        out_specs=pl.BlockSpec((ATT_PAIRS, tq, LANES), lambda b, i: (0, b * nq + i, 0)),
        out_shape=jax.ShapeDtypeStruct((ATT_PAIRS, nb * seq, LANES), BF16),
        scratch_shapes=[
            pltpu.VMEM((ATT_KV_HEADS, seq, LANES), BF16),
            pltpu.VMEM((ATT_KV_HEADS, 2, seq, LANES), BF16),
        ],
        compiler_params=pltpu.CompilerParams(
            dimension_semantics=("arbitrary", "arbitrary"), vmem_limit_bytes=VMEM_LIMIT_BYTES),
        name="attn",
    )(z4, z4, z4, z4, z4, z4, cos, sin, cos, sin,
      jnp.tile(q_gain, reps).reshape(1, LANES), jnp.tile(k_gain, reps).reshape(1, LANES),
      jnp.tile(out_gain, reps).reshape(1, LANES), gmat)


def _outffn_kernel(ah_ref, aa_ref, x_ref, mod_ref, g_post_mix_ref, g_pre_ff_ref, g_post_ff_ref,
                   wo_ref, w1_ref, w2_ref, o_ref):
    gate1 = mod_ref[2:3, :]
    shift2 = mod_ref[3:4, :]
    scale2 = mod_ref[4:5, :]
    gate2 = mod_ref[5:6, :]
    a = jnp.concatenate([ah_ref[j] for j in range(ah_ref.shape[0])]
                        + [aa_ref[j] for j in range(aa_ref.shape[0])], axis=1)
    y = _dot(a, wo_ref[...])
    x1 = x_ref[...] + gate1 * _row_rms(y, g_post_mix_ref[...])
    h = (_row_rms(x1, g_pre_ff_ref[...]) * (1.0 + scale2) + shift2).astype(BF16)
    d_ff = w1_ref.shape[1]
    fc = 1024
    acc = None
    for c0 in range(0, d_ff, fc):
        u = jnp.maximum(_dot(h, w1_ref[:, c0:c0 + fc]), 0.0)
        part = _dot((u * u).astype(BF16), w2_ref[c0:c0 + fc, :])
        acc = part if acc is None else acc + part
    o_ref[...] = x1 + gate2 * _row_rms(acc, g_post_ff_ref[...])


def _outffn_call(ah, aa, x2d, mod_l, g_post_mix, g_pre_ff, g_post_ff, wo_bf, w1_bf, w2_bf, seq, tm):
    rows, d = x2d.shape
    d_ff = w1_bf.shape[1]
    tiles_per_seq = seq // tm
    vec = pl.BlockSpec((1, d), lambda i: (0, 0))

    def resident(shape):
        return pl.BlockSpec(shape, lambda i: (0, 0), pipeline_mode=pl.Buffered(1))

    return pl.pallas_call(
        _outffn_kernel,
        grid=(rows // tm,),
        in_specs=[
            pl.BlockSpec((ah.shape[0], tm, LANES), lambda i: (0, i, 0)),
            pl.BlockSpec((aa.shape[0], tm, LANES), lambda i: (0, i, 0)),
            pl.BlockSpec((tm, d), lambda i: (i, 0)),
            pl.BlockSpec((None, N_MOD, d), lambda i: (i // tiles_per_seq, 0, 0)),
            vec, vec, vec,
            resident((d, d)), resident((d, d_ff)), resident((d_ff, d)),
        ],
        out_specs=pl.BlockSpec((tm, d), lambda i: (i, 0)),
        out_shape=jax.ShapeDtypeStruct((rows, d), F32),
        compiler_params=pltpu.CompilerParams(
            dimension_semantics=("arbitrary",), vmem_limit_bytes=VMEM_LIMIT_BYTES),
        name="outffn",
    )(ah, aa, x2d, mod_l, g_post_mix, g_pre_ff, g_post_ff, wo_bf, w1_bf, w2_bf)


def kernel(x, c, w_ada, ada_layer_bias, hg_lb_logits, pre_mix, post_mix, w_in, w_out, hg_out_gain,
           q_gain, k_gain, att_out_gain, pre_ff, post_ff, w_ff1, w_ff2):
    nb, seq, d = x.shape
    depth = w_in.shape[0]
    assert w_in.shape[2] == N_COL_BLOCKS * LANES and d == HG_WIDTH + ATT_WIDTH
    tm = min(512, seq)
    tq = min(256, seq)

    mod, lb_table = _cond_call(c, w_ada, ada_layer_bias, hg_lb_logits)
    mod = mod.reshape(depth, nb, N_MOD, d)
    cos, sin = _rope_tables(seq)

    x2d = x.reshape(nb * seq, d)
    for l in range(depth):
        z = _inproj_call(x2d, mod[l], pre_mix[l].reshape(1, d), w_in[l].astype(BF16), seq, tm)
        z4 = z.reshape(N_COL_BLOCKS, nb, seq, LANES)
        y_hgrn = _hgrn_call(z4, lb_table[l], hg_out_gain[l])
        y_att = _attn_call(z4, cos, sin, q_gain[l], k_gain[l], att_out_gain[l], tq)
        x2d = _outffn_call(y_hgrn, y_att, x2d, mod[l], post_mix[l].reshape(1, d),
                           pre_ff[l].reshape(1, d), post_ff[l].reshape(1, d),
                           w_out[l].astype(BF16), w_ff1[l].astype(BF16), w_ff2[l].astype(BF16), seq, tm)
    return x2d.reshape(nb, seq, d)
```

```python
import functools

import numpy as np
import jax
import jax.numpy as jnp
from jax import lax
from jax.experimental import pallas as pl
from jax.experimental.pallas import tpu as pltpu

F32 = jnp.float32
BF16 = jnp.bfloat16

LANES = 128
VMEM_LIMIT_BYTES = 56 * 1024 * 1024

EPS = 1e-6
N_MOD = 6
HG_HEADS = 4
HG_HEAD_DIM = 128
HG_WIDTH = HG_HEADS * HG_HEAD_DIM
HG_CHUNK = 64
HG_GROUP = 8
HG_PREP_ROWS = 256
LOG2E = 1.4426950408889634
HG_LEVEL_HALVES = (32, 16, 8, 4, 2, 1)
ATT_HEADS = 8
ATT_HEAD_DIM = 64
ATT_KV_HEADS = 2
ATT_WIDTH = ATT_HEADS * ATT_HEAD_DIM
KV_WIDTH = ATT_KV_HEADS * ATT_HEAD_DIM
GRID_W = 64
ROPE_THETA = 10000.0

CB_HQ, CB_FFW, CB_FBW, CB_HI, CB_HG = 0, 4, 8, 12, 16
CB_AQ, CB_AK, CB_AV = 20, 24, 25
N_COL_BLOCKS = 26
ATT_PAIRS = ATT_WIDTH // LANES
PAIRS_PER_KV = ATT_PAIRS // ATT_KV_HEADS


def _dot(a, b):
    return jnp.dot(a, b, preferred_element_type=F32)


def _dot_nt(a, b):
    return lax.dot_general(a, b, (((1,), (1,)), ((), ())), preferred_element_type=F32)


def _split2(x):
    hi = x.astype(BF16)
    lo = (x - hi.astype(F32)).astype(BF16)
    return hi, lo


def _split3(x):
    hi = x.astype(BF16)
    r1 = x - hi.astype(F32)
    mid = r1.astype(BF16)
    lo = (r1 - mid.astype(F32)).astype(BF16)
    return hi, mid, lo


def _sigmoid(x):
    t = jnp.exp(-jnp.abs(x))
    return jnp.where(x >= 0, 1.0, t) / (1.0 + t)


def _row_rms(x, gain):
    ms = jnp.mean(x * x, axis=-1, keepdims=True)
    return x * lax.rsqrt(ms + EPS) * gain


def _group_mean_sq(x, gmat):
    w = x.shape[1]
    r = _dot(jnp.concatenate(_split2(x * x), axis=1), gmat)
    return r[:, :w] + r[:, w:]


def _cond_kernel(c_ref, w_ref, bias_ref, lbl_ref, mod_ref, lb_ref):
    depth = bias_ref.shape[0]
    c = c_ref[...]
    s = c * _sigmoid(c)
    s_hi = s.astype(BF16).astype(F32)
    s_lo = s - s_hi
    w_hi, w_lo = _split2(w_ref[...])
    nb = c.shape[0]
    r1 = _dot(jnp.concatenate([s_hi, s_lo], axis=0).astype(BF16), w_hi)
    r2 = _dot(jnp.concatenate([s_hi, s_hi], axis=0).astype(BF16), w_lo)
    cond = r1[:nb] + r1[nb:] + r2[:nb]
    for l in range(depth):
        mod_ref[l] = cond + bias_ref[l:l + 1, :]

    rows = [lbl_ref[l:l + 1, :] for l in range(depth)]
    m = functools.reduce(jnp.maximum, rows)
    es = [jnp.exp(r - m) for r in rows]
    tot = functools.reduce(lambda a, b: a + b, es)
    cum = None
    first = None
    for l in range(depth):
        p = es[l] / tot
        cum = p if cum is None else cum + p
        if first is None:
            first = cum
        lb_ref[l:l + 1, :] = cum - first


def _cond_call(c, w_ada, ada_layer_bias, hg_lb_logits):
    nb, d = c.shape
    n_out = w_ada.shape[1]
    depth = ada_layer_bias.shape[0]
    tn = n_out // 4
    return pl.pallas_call(
        _cond_kernel,
        grid=(n_out // tn,),
        in_specs=[
            pl.BlockSpec((nb, d), lambda j: (0, 0)),
            pl.BlockSpec((d, tn), lambda j: (0, j)),
            pl.BlockSpec((depth, tn), lambda j: (0, j)),
            pl.BlockSpec(hg_lb_logits.shape, lambda j: (0, 0)),
        ],
        out_specs=[
            pl.BlockSpec((depth, nb, tn), lambda j: (0, 0, j)),
            pl.BlockSpec(hg_lb_logits.shape, lambda j: (0, 0)),
        ],
        out_shape=[
            jax.ShapeDtypeStruct((depth, nb, n_out), F32),
            jax.ShapeDtypeStruct(hg_lb_logits.shape, F32),
        ],
        compiler_params=pltpu.CompilerParams(
            dimension_semantics=("arbitrary",), vmem_limit_bytes=VMEM_LIMIT_BYTES),
        name="cond",
    )(c, w_ada, ada_layer_bias, hg_lb_logits)


def _inproj_kernel(x_ref, mod_ref, gain_ref, w_ref, z_ref):
    shift = mod_ref[0:1, :]
    scale = mod_ref[1:2, :]
    n_blocks = z_ref.shape[0]
    half = x_ref.shape[0] // 2
    for r0 in (0, half):
        rows = pl.ds(r0, half)
        h = _row_rms(x_ref[rows, :], gain_ref[...]) * (1.0 + scale) + shift
        hb = h.astype(BF16)
        for cb in range(0, n_blocks, 2):
            z = _dot(hb, w_ref[:, cb * LANES:(cb + 2) * LANES])
            z_ref[cb, rows, :] = z[:, :LANES]
            z_ref[cb + 1, rows, :] = z[:, LANES:]


def _inproj_call(x2d, mod_l, gain, w_in_bf, seq, tm):
    rows, d = x2d.shape
    n_cols = w_in_bf.shape[1]
    n_blocks = n_cols // LANES
    tiles_per_seq = seq // tm
    return pl.pallas_call(
        _inproj_kernel,
        grid=(rows // tm,),
        in_specs=[
            pl.BlockSpec((tm, d), lambda i: (i, 0)),
            pl.BlockSpec((None, N_MOD, d), lambda i: (i // tiles_per_seq, 0, 0)),
            pl.BlockSpec((1, d), lambda i: (0, 0)),
            pl.BlockSpec((d, n_cols), lambda i: (0, 0), pipeline_mode=pl.Buffered(1)),
        ],
        out_specs=pl.BlockSpec((n_blocks, tm, LANES), lambda i: (0, i, 0)),
        out_shape=jax.ShapeDtypeStruct((n_blocks, rows, LANES), F32),
        compiler_params=pltpu.CompilerParams(
            dimension_semantics=("arbitrary",), vmem_limit_bytes=VMEM_LIMIT_BYTES),
        name="inproj",
    )(x2d, mod_l, gain, w_in_bf)


def _hgrn_tables():
    c = HG_CHUNK
    n = HG_PREP_ROWS
    tt = np.arange(n)[:, None]
    ss = np.arange(n)[None, :]
    same_chunk = (tt // c) == (ss // c)
    tri = np.stack([same_chunk & (ss <= tt), same_chunk & (ss >= tt)]).astype(np.float32)
    t = np.arange(c)[:, None]
    s = np.arange(c)[None, :]
    fwd = []
    for half in HG_LEVEL_HALVES:
        same = (t // (2 * half)) == (s // (2 * half))
        t_low = (t // half) % 2 == 1
        s_up = (s // half) % 2 == 0
        fwd.append(same & t_low & s_up)
    fwd.append(t == s)
    fwd = np.stack(fwd).astype(np.float32)
    masks = np.stack([fwd, fwd.transpose(0, 2, 1)])
    return tri, np.tile(masks, (1, 1, 1, 2))


def _hgrn_kernel(q_ref, ffw_ref, fbw_ref, i_ref, g_ref, lb_ref, gain_ref, tri_ref, mask_ref,
                 y_ref, st_scr, o_scr, qs_scr, k_scr, b_scr, v_scr, vt_scr):
    seq = q_ref.shape[0]
    n_chunks = seq // HG_CHUNK
    c = HG_CHUNK
    grp = HG_GROUP
    slab = min(HG_PREP_ROWS, seq)
    n_sub = 2 if seq % (2 * slab) == 0 else 1
    lb = lb_ref[...]
    one_m_lb = 1.0 - lb
    row = lax.broadcasted_iota(jnp.int32, (c, LANES), 0)
    sub = lax.broadcasted_iota(jnp.int32, (8, LANES), 0)

    st_scr[...] = jnp.zeros(st_scr.shape, F32)

    def prep(i, carry):
        work = []
        for j in range(n_sub):
            rows = pl.ds(pl.multiple_of((i * n_sub + j) * slab, slab), slab)
            zq = q_ref[rows, :]
            qs_scr[rows, :] = zq * _sigmoid(zq)
            v = i_ref[rows, :]
            v_scr[rows, :] = v.astype(BF16)
            for jc in range(slab // c):
                vt_scr[(i * n_sub + j) * (slab // c) + jc] = v[jc * c:(jc + 1) * c].T.astype(BF16)
            for d, f_ref in enumerate((ffw_ref, fbw_ref)):
                zf = f_ref[rows, :]
                t = jnp.exp(-jnp.abs(zf))
                neg = zf < 0
                inv = 1.0 / (1.0 + t)
                num = jnp.where(neg, lb + t, lb * t + 1.0)
                g = jnp.where(num > 0, jnp.log(num * inv), zf) * LOG2E
                k_scr[d, rows, :] = one_m_lb * (jnp.where(neg, 1.0, t) * inv)
                work.append((d, rows, jnp.concatenate(_split3(g), axis=1)))
        sums = [_dot(tri_ref[d, :slab, :slab], g3) for d, _, g3 in work]
        for (d, rows, _), b3 in zip(work, sums):
            b_scr[d, rows, :] = b3[:, :LANES] + b3[:, LANES:2 * LANES] + b3[:, 2 * LANES:]
        return carry

    lax.fori_loop(0, seq // (slab * n_sub), prep, 0)

    def mid_reference(d, r0, b, half):
        def brow(r, n):
            return jnp.broadcast_to(b_scr[d, pl.ds(r0 + r, 1), :], (n, LANES))

        if half == c // 2:
            return brow(half, c)
        if half >= 4:
            return jnp.concatenate(
                [brow(base + half, 2 * half) for base in range(0, c, 2 * half)], axis=0)
        if half == 2:
            return jnp.concatenate(
                [jnp.where(sub < 4, brow(base + 2, 8), brow(base + 6, 8))
                 for base in range(0, c, 8)], axis=0)
        return jnp.where((row & 1) == 0, pltpu.roll(b, c - 1, axis=0), b)

    def body(i, carry):
        chunk_ids = [i * grp + j for j in range(grp)]
        chunk_ids += [n_chunks - 1 - (i * grp + j) for j in range(grp)]
        insts = [(slot // grp, pl.multiple_of(cid * c, c)) for slot, cid in enumerate(chunk_ids)]
        pairs = [(d, (s, s + 1)) for d in range(2) for s in range(d * grp, (d + 1) * grp, 2)]

        qs_l = [qs_scr[pl.ds(r0, c), :] for _, r0 in insts]
        k_l = [k_scr[d, pl.ds(r0, c), :] for d, r0 in insts]
        v_l = [v_scr[pl.ds(r0, c), :] for _, r0 in insts]
        b_l = [b_scr[d, pl.ds(r0, c), :] for d, r0 in insts]

        o_l = [None] * len(insts)
        for d in range(2):
            st = st_scr[d]
            for slot in range(d * grp, (d + 1) * grp):
                edge = c - 1 if d == 0 else 0
                b_edge = b_scr[d, pl.ds(insts[slot][1] + edge, 1), :]
                qe = (qs_l[slot] * jnp.exp2(b_l[slot])).astype(BF16)
                kd = (k_l[slot] * jnp.exp2(b_edge - b_l[slot])).astype(BF16)
                o_l[slot] = _dot(qe, st.T.astype(BF16))
                st = st * jnp.exp2(b_edge) + _dot(vt_scr[chunk_ids[slot]], kd)
            st_scr[d] = st

        zero = jnp.zeros((c, LANES), BF16)

        def block_diag(xa, xb):
            return jnp.concatenate([jnp.concatenate([xa, zero], axis=1),
                                    jnp.concatenate([zero, xb], axis=1)], axis=0)

        def pair_scores(xa_q, xa_k, xb_q, xb_k):
            return _dot_nt(jnp.concatenate([xa_q, xb_q], axis=1), block_diag(xa_k, xb_k))

        n_lvl = len(HG_LEVEL_HALVES)
        sc_l = []
        for d, (sa, sb) in pairs:
            sc_l.append(pair_scores(qs_l[sa].astype(BF16), k_l[sa].astype(BF16),
                                    qs_l[sb].astype(BF16), k_l[sb].astype(BF16))
                        * mask_ref[d, n_lvl])
        for lvl, half in enumerate(HG_LEVEL_HALVES):
            for p, (d, slots) in enumerate(pairs):
                xs = []
                for slot in slots:
                    b = b_l[slot]
                    q_rows = ((row & half) != 0) if d == 0 else ((row & half) == 0)
                    sign = jnp.where(q_rows, 1.0, -1.0)
                    e = jnp.exp2((b - mid_reference(d, insts[slot][1], b, half)) * sign)
                    xs.append((jnp.where(q_rows, qs_l[slot], k_l[slot]) * e).astype(BF16))
                sc_l[p] = sc_l[p] + pair_scores(xs[0], xs[0], xs[1], xs[1]) * mask_ref[d, lvl]

        for p, (d, (sa, sb)) in enumerate(pairs):
            o_intra = _dot(sc_l[p].astype(BF16),
                           block_diag(v_l[sa], v_l[sb]))
            o_scr[d, pl.ds(insts[sa][1], c), :] = o_l[sa] + o_intra[:, :LANES]
            o_scr[d, pl.ds(insts[sb][1], c), :] = o_l[sb] + o_intra[:, LANES:]
        return carry

    lax.fori_loop(0, n_chunks // grp, body, 0)

    gain = gain_ref[...]

    def fin(i, carry):
        for j in range(n_sub):
            rows = pl.ds(pl.multiple_of((i * n_sub + j) * slab, slab), slab)
            o = o_scr[0, rows, :] + o_scr[1, rows, :]
            zg = g_ref[rows, :]
            y = _row_rms(o, gain) * (zg * _sigmoid(zg))
            y_ref[rows, :] = y.astype(y_ref.dtype)
        return carry

    lax.fori_loop(0, seq // (slab * n_sub), fin, 0)


def _hgrn_call(z4, lb_l, out_gain):
    _, nb, seq, _ = z4.shape
    tri, masks = _hgrn_tables()

    def zspec(cb0):
        return pl.BlockSpec((None, None, seq, LANES), lambda b, h: (cb0 + h, b, 0, 0))

    y = pl.pallas_call(
        _hgrn_kernel,
        grid=(nb, HG_HEADS),
        in_specs=[
            zspec(CB_HQ), zspec(CB_FFW), zspec(CB_FBW), zspec(CB_HI), zspec(CB_HG),
            pl.BlockSpec((None, 1, LANES), lambda b, h: (h, 0, 0)),
            pl.BlockSpec((1, LANES), lambda b, h: (0, 0)),
            pl.BlockSpec(tri.shape, lambda b, h: (0, 0, 0)),
            pl.BlockSpec(masks.shape, lambda b, h: (0, 0, 0, 0)),
        ],
        out_specs=pl.BlockSpec((None, None, seq, LANES), lambda b, h: (h, b, 0, 0)),
        out_shape=jax.ShapeDtypeStruct((HG_HEADS, nb, seq, LANES), BF16),
        scratch_shapes=[
            pltpu.VMEM((2, HG_HEAD_DIM, HG_HEAD_DIM), F32),
            pltpu.VMEM((2, seq, LANES), F32),
            pltpu.VMEM((seq, LANES), F32),
            pltpu.VMEM((2, seq, LANES), F32),
            pltpu.VMEM((2, seq, LANES), F32),
            pltpu.VMEM((seq, LANES), BF16),
            pltpu.VMEM((seq // HG_CHUNK, HG_HEAD_DIM, HG_CHUNK), BF16),
        ],
        compiler_params=pltpu.CompilerParams(
            dimension_semantics=("arbitrary", "arbitrary"), vmem_limit_bytes=VMEM_LIMIT_BYTES),
        name="hgrn",
    )(z4, z4, z4, z4, z4, lb_l.reshape(HG_HEADS, 1, LANES), out_gain.reshape(1, LANES),
      jnp.asarray(tri, BF16), jnp.asarray(masks, F32))
    return y.reshape(HG_HEADS, nb * seq, LANES)


def _rope_tables(seq):
    pos = jnp.arange(seq)
    row = (pos // GRID_W).astype(F32)
    col = (pos % GRID_W).astype(F32)
    half = ATT_HEAD_DIM // 2
    inv_freq = 1.0 / (ROPE_THETA ** (jnp.arange(0, half, 2, dtype=F32) / half))
    ang_r = row[:, None] * inv_freq
    ang_c = col[:, None] * inv_freq
    cos = jnp.concatenate([jnp.cos(ang_r)] * 2 + [jnp.cos(ang_c)] * 2, axis=-1)
    sin = jnp.concatenate([-jnp.sin(ang_r), jnp.sin(ang_r), -jnp.sin(ang_c), jnp.sin(ang_c)], axis=-1)
    reps = LANES // ATT_HEAD_DIM
    return jnp.tile(cos, (1, reps)), jnp.tile(sin, (1, reps))


def _rope(x, cos, sin, lane):
    quarter = ATT_HEAD_DIM // 4
    fwd = pltpu.roll(x, LANES - quarter, axis=1)
    bwd = pltpu.roll(x, quarter, axis=1)
    partner = jnp.where((lane & quarter) == 0, fwd, bwd)
    return x * cos + partner * sin


def _attn_kernel(q0_ref, q1_ref, q2_ref, q3_ref, k_ref, v_ref, cosq_ref, sinq_ref, cosk_ref, sink_ref,
                 qg_ref, kg_ref, og_ref, gmat_ref, y_ref, kext_scr, vext_scr):
    seq = k_ref.shape[0]
    tq = q0_ref.shape[0]
    gmat = gmat_ref[...]

    @pl.when(pl.program_id(1) == 0)
    def _prepare_kv():
        slab = min(512, seq)
        lane = lax.broadcasted_iota(jnp.int32, (slab, LANES), 1)
        first = lane < ATT_HEAD_DIM
        for r0 in range(0, seq, slab):
            rs = slice(r0, r0 + slab)
            k = k_ref[rs, :]
            kn = k * lax.rsqrt(_group_mean_sq(k, gmat) + EPS) * kg_ref[...]
            kr = _rope(kn, cosk_ref[rs, :], sink_ref[rs, :], lane)
            kr_sw = pltpu.roll(kr, ATT_HEAD_DIM, axis=1)
            kext_scr[0, rs, :] = jnp.where(first, kr, kr_sw).astype(BF16)
            kext_scr[1, rs, :] = jnp.where(first, kr_sw, kr).astype(BF16)
            v = v_ref[rs, :]
            v_sw = pltpu.roll(v, ATT_HEAD_DIM, axis=1)
            vext_scr[0, 0, rs, :] = jnp.where(first, v, 1.0).astype(BF16)
            vext_scr[0, 1, rs, :] = jnp.where(first, 1.0, v_sw).astype(BF16)
            vext_scr[1, 0, rs, :] = jnp.where(first, v_sw, 1.0).astype(BF16)
            vext_scr[1, 1, rs, :] = jnp.where(first, 1.0, v).astype(BF16)

    lane = lax.broadcasted_iota(jnp.int32, (tq, LANES), 1)
    first = lane < ATT_HEAD_DIM
    cosq = cosq_ref[...]
    sinq = sinq_ref[...]
    sm_scale = ATT_HEAD_DIM ** -0.5 * LOG2E
    q_refs = (q0_ref, q1_ref, q2_ref, q3_ref)
    heads = [(pair, parity) for pair in range(ATT_PAIRS) for parity in range(2)]

    def scores(pair, parity):
        if parity == 0:
            q = q_refs[pair][...]
            qn = q * lax.rsqrt(_group_mean_sq(q, gmat) + EPS) * (qg_ref[...] * sm_scale)
            scores.qr = _rope(qn, cosq, sinq, lane)
        keep = first if parity == 0 else jnp.logical_not(first)
        qm = jnp.where(keep, scores.qr, 0.0).astype(BF16)
        return _dot_nt(qm, kext_scr[pair // PAIRS_PER_KV])

    halves = []

    def values(pair, parity, p):
        ov = _dot(p, vext_scr[pair // PAIRS_PER_KV, parity])
        halves.append(ov / pltpu.roll(ov, ATT_HEAD_DIM, axis=1))
        if parity == 1:
            o = jnp.where(first, halves[0], halves[1])
            y = o * lax.rsqrt(_group_mean_sq(o, gmat) + EPS) * og_ref[...]
            y_ref[pair] = y.astype(y_ref.dtype)
            halves.clear()

    s = scores(*heads[0])
    pending = None
    for idx, head in enumerate(heads):
        s_next = scores(*heads[idx + 1]) if idx + 1 < len(heads) else None
        if pending is not None:
            values(*pending)
        m = jnp.max(s, axis=-1, keepdims=True)
        pending = (*head, jnp.exp2(s - m).astype(BF16))
        s = s_next
    values(*pending)


def _attn_call(z4, cos, sin, q_gain, k_gain, out_gain, tq):
    _, nb, seq, _ = z4.shape
    nq = seq // tq
    reps = LANES // ATT_HEAD_DIM
    lanes2 = np.arange(2 * LANES)
    grp = (lanes2[:, None] // ATT_HEAD_DIM) == (lanes2[None, :] // ATT_HEAD_DIM)
    gmat = jnp.asarray(grp.astype(np.float32) / ATT_HEAD_DIM, BF16)

    def qspec(pair):
        return pl.BlockSpec((None, None, tq, LANES), lambda b, i: (CB_AQ + pair, b, i, 0))

    def full(cb):
        return pl.BlockSpec((None, None, seq, LANES), lambda b, i: (cb, b, 0, 0))

    vec = pl.BlockSpec((1, LANES), lambda b, i: (0, 0))
    return pl.pallas_call(
        _attn_kernel,
        grid=(nb, nq),
        in_specs=[
            qspec(0), qspec(1), qspec(2), qspec(3), full(CB_AK), full(CB_AV),
            pl.BlockSpec((tq, LANES), lambda b, i: (i, 0)),
            pl.BlockSpec((tq, LANES), lambda b, i: (i, 0)),
            pl.BlockSpec((seq, LANES), lambda b, i: (0, 0)),
            pl.BlockSpec((seq, LANES), lambda b, i: (0, 0)),
            vec, vec, vec,
            pl.BlockSpec((2 * LANES, 2 * LANES), lambda b, i: (0, 0)),
        ],
        out_specs=pl.BlockSpec((ATT_PAIRS, tq, LANES), lambda b, i: (0, b * nq + i, 0)),
        out_shape=jax.ShapeDtypeStruct((ATT_PAIRS, nb * seq, LANES), BF16),
        scratch_shapes=[
            pltpu.VMEM((ATT_KV_HEADS, seq, LANES), BF16),
            pltpu.VMEM((ATT_KV_HEADS, 2, seq, LANES), BF16),
        ],
        compiler_params=pltpu.CompilerParams(
            dimension_semantics=("arbitrary", "arbitrary"), vmem_limit_bytes=VMEM_LIMIT_BYTES),
        name="attn",
    )(z4, z4, z4, z4, z4, z4, cos, sin, cos, sin,
      jnp.tile(q_gain, reps).reshape(1, LANES), jnp.tile(k_gain, reps).reshape(1, LANES),
      jnp.tile(out_gain, reps).reshape(1, LANES), gmat)


def _outffn_kernel(ah_ref, aa_ref, x_ref, mod_ref, g_post_mix_ref, g_pre_ff_ref, g_post_ff_ref,
                   wo_ref, w1_ref, w2_ref, o_ref):
    gate1 = mod_ref[2:3, :]
    shift2 = mod_ref[3:4, :]
    scale2 = mod_ref[4:5, :]
    gate2 = mod_ref[5:6, :]
    d_ff = w1_ref.shape[1]
    fc = 1024
    half = x_ref.shape[0] // 2
    halves = [pl.ds(r0, half) for r0 in (0, half)]
    ys = []
    for rows in halves:
        a = jnp.concatenate([ah_ref[j, rows, :] for j in range(ah_ref.shape[0])]
                            + [aa_ref[j, rows, :] for j in range(aa_ref.shape[0])], axis=1)
        ys.append(_dot(a, wo_ref[...]))
    for rows, y in zip(halves, ys):
        x1 = x_ref[rows, :] + gate1 * _row_rms(y, g_post_mix_ref[...])
        h = (_row_rms(x1, g_pre_ff_ref[...]) * (1.0 + scale2) + shift2).astype(BF16)
        acc = None
        for c0 in range(0, d_ff, fc):
            u = jnp.maximum(_dot(h, w1_ref[:, c0:c0 + fc]), 0.0)
            part = _dot((u * u).astype(BF16), w2_ref[c0:c0 + fc, :])
            acc = part if acc is None else acc + part
        o_ref[rows, :] = x1 + gate2 * _row_rms(acc, g_post_ff_ref[...])


def _outffn_call(ah, aa, x2d, mod_l, g_post_mix, g_pre_ff, g_post_ff, wo_bf, w1_bf, w2_bf, seq, tm):
    rows, d = x2d.shape
    d_ff = w1_bf.shape[1]
    tiles_per_seq = seq // tm
    vec = pl.BlockSpec((1, d), lambda i: (0, 0))

    def resident(shape):
        return pl.BlockSpec(shape, lambda i: (0, 0), pipeline_mode=pl.Buffered(1))

    return pl.pallas_call(
        _outffn_kernel,
        grid=(rows // tm,),
        in_specs=[
            pl.BlockSpec((ah.shape[0], tm, LANES), lambda i: (0, i, 0)),
            pl.BlockSpec((aa.shape[0], tm, LANES), lambda i: (0, i, 0)),
            pl.BlockSpec((tm, d), lambda i: (i, 0)),
            pl.BlockSpec((None, N_MOD, d), lambda i: (i // tiles_per_seq, 0, 0)),
            vec, vec, vec,
            resident((d, d)), resident((d, d_ff)), resident((d_ff, d)),
        ],
        out_specs=pl.BlockSpec((tm, d), lambda i: (i, 0)),
        out_shape=jax.ShapeDtypeStruct((rows, d), F32),
        compiler_params=pltpu.CompilerParams(
            dimension_semantics=("arbitrary",), vmem_limit_bytes=VMEM_LIMIT_BYTES),
        name="outffn",
    )(ah, aa, x2d, mod_l, g_post_mix, g_pre_ff, g_post_ff, wo_bf, w1_bf, w2_bf)


def kernel(x, c, w_ada, ada_layer_bias, hg_lb_logits, pre_mix, post_mix, w_in, w_out, hg_out_gain,
           q_gain, k_gain, att_out_gain, pre_ff, post_ff, w_ff1, w_ff2):
    nb, seq, d = x.shape
    depth = w_in.shape[0]
    assert w_in.shape[2] == N_COL_BLOCKS * LANES and d == HG_WIDTH + ATT_WIDTH
    tm = min(512, seq)
    tq = min(256, seq)

    mod, lb_table = _cond_call(c, w_ada, ada_layer_bias, hg_lb_logits)
    mod = mod.reshape(depth, nb, N_MOD, d)
    cos, sin = _rope_tables(seq)

    x2d = x.reshape(nb * seq, d)
    for l in range(depth):
        z = _inproj_call(x2d, mod[l], pre_mix[l].reshape(1, d), w_in[l].astype(BF16), seq, tm)
        z4 = z.reshape(N_COL_BLOCKS, nb, seq, LANES)
        y_hgrn = _hgrn_call(z4, lb_table[l], hg_out_gain[l])
        y_att = _attn_call(z4, cos, sin, q_gain[l], k_gain[l], att_out_gain[l], tq)
        x2d = _outffn_call(y_hgrn, y_att, x2d, mod[l], post_mix[l].reshape(1, d),
                           pre_ff[l].reshape(1, d), post_ff[l].reshape(1, d),
                           w_out[l].astype(BF16), w_ff1[l].astype(BF16), w_ff2[l].astype(BF16), seq, tm)
    return x2d.reshape(nb, seq, d)
```

```python
import functools

import numpy as np
import jax
import jax.numpy as jnp
from jax import lax
from jax.experimental import pallas as pl
from jax.experimental.pallas import tpu as pltpu

F32 = jnp.float32
BF16 = jnp.bfloat16

LANES = 128
VMEM_LIMIT_BYTES = 56 * 1024 * 1024
ROW_PART = 256

EPS = 1e-6
N_MOD = 6
HG_HEADS = 4
HG_HEAD_DIM = 128
HG_WIDTH = HG_HEADS * HG_HEAD_DIM
HG_CHUNK = 64
HG_GROUP = 8
HG_PREP_ROWS = 256
LOG2E = 1.4426950408889634
HG_LEVEL_HALVES = (32, 16, 8, 4, 2, 1)
ATT_HEADS = 8
ATT_HEAD_DIM = 64
ATT_KV_HEADS = 2
ATT_WIDTH = ATT_HEADS * ATT_HEAD_DIM
KV_WIDTH = ATT_KV_HEADS * ATT_HEAD_DIM
GRID_W = 64
ROPE_THETA = 10000.0

CB_HQ, CB_FFW, CB_FBW, CB_HI, CB_HG = 0, 4, 8, 12, 16
CB_AQ, CB_AK, CB_AV = 20, 24, 25
N_COL_BLOCKS = 26
ATT_PAIRS = ATT_WIDTH // LANES
PAIRS_PER_KV = ATT_PAIRS // ATT_KV_HEADS


def _dot(a, b):
    return jnp.dot(a, b, preferred_element_type=F32)


def _dot_nt(a, b):
    return lax.dot_general(a, b, (((1,), (1,)), ((), ())), preferred_element_type=F32)


def _split2(x):
    hi = x.astype(BF16)
    lo = (x - hi.astype(F32)).astype(BF16)
    return hi, lo


def _split3(x):
    hi = x.astype(BF16)
    r1 = x - hi.astype(F32)
    mid = r1.astype(BF16)
    lo = (r1 - mid.astype(F32)).astype(BF16)
    return hi, mid, lo


def _sigmoid(x):
    t = jnp.exp(-jnp.abs(x))
    return jnp.where(x >= 0, 1.0, t) / (1.0 + t)


def _row_rms(x, gain):
    ms = jnp.mean(x * x, axis=-1, keepdims=True)
    return x * lax.rsqrt(ms + EPS) * gain


def _group_mean_sq(x, gmat):
    w = x.shape[1]
    r = _dot(jnp.concatenate(_split2(x * x), axis=1), gmat)
    return r[:, :w] + r[:, w:]


def _cond_kernel(c_ref, w_ref, bias_ref, lbl_ref, mod_ref, lb_ref):
    depth = bias_ref.shape[0]
    c = c_ref[...]
    s = c * _sigmoid(c)
    s_hi = s.astype(BF16).astype(F32)
    s_lo = s - s_hi
    w_hi, w_lo = _split2(w_ref[...])
    nb = c.shape[0]
    r1 = _dot(jnp.concatenate([s_hi, s_lo], axis=0).astype(BF16), w_hi)
    r2 = _dot(jnp.concatenate([s_hi, s_hi], axis=0).astype(BF16), w_lo)
    cond = r1[:nb] + r1[nb:] + r2[:nb]
    for l in range(depth):
        mod_ref[l] = cond + bias_ref[l:l + 1, :]

    rows = [lbl_ref[l:l + 1, :] for l in range(depth)]
    m = functools.reduce(jnp.maximum, rows)
    es = [jnp.exp(r - m) for r in rows]
    tot = functools.reduce(lambda a, b: a + b, es)
    cum = None
    first = None
    for l in range(depth):
        p = es[l] / tot
        cum = p if cum is None else cum + p
        if first is None:
            first = cum
        lb_ref[l:l + 1, :] = cum - first


def _cond_call(c, w_ada, ada_layer_bias, hg_lb_logits):
    nb, d = c.shape
    n_out = w_ada.shape[1]
    depth = ada_layer_bias.shape[0]
    tn = n_out // 4
    return pl.pallas_call(
        _cond_kernel,
        grid=(n_out // tn,),
        in_specs=[
            pl.BlockSpec((nb, d), lambda j: (0, 0)),
            pl.BlockSpec((d, tn), lambda j: (0, j)),
            pl.BlockSpec((depth, tn), lambda j: (0, j)),
            pl.BlockSpec(hg_lb_logits.shape, lambda j: (0, 0)),
        ],
        out_specs=[
            pl.BlockSpec((depth, nb, tn), lambda j: (0, 0, j)),
            pl.BlockSpec(hg_lb_logits.shape, lambda j: (0, 0)),
        ],
        out_shape=[
            jax.ShapeDtypeStruct((depth, nb, n_out), F32),
            jax.ShapeDtypeStruct(hg_lb_logits.shape, F32),
        ],
        compiler_params=pltpu.CompilerParams(
            dimension_semantics=("arbitrary",), vmem_limit_bytes=VMEM_LIMIT_BYTES),
        name="cond",
    )(c, w_ada, ada_layer_bias, hg_lb_logits)


def _inproj_kernel(x_ref, mod_ref, gain_ref, w_ref, z_ref):
    shift = mod_ref[0:1, :]
    scale = mod_ref[1:2, :]
    n_blocks = z_ref.shape[0]
    part_rows = min(ROW_PART, x_ref.shape[0])
    for r0 in range(0, x_ref.shape[0], part_rows):
        rows = pl.ds(r0, part_rows)
        h = _row_rms(x_ref[rows, :], gain_ref[...]) * (1.0 + scale) + shift
        hb = h.astype(BF16)
        for cb in range(0, n_blocks, 2):
            z = _dot(hb, w_ref[:, cb * LANES:(cb + 2) * LANES])
            z_ref[cb, rows, :] = z[:, :LANES]
            z_ref[cb + 1, rows, :] = z[:, LANES:]


def _inproj_call(layer, x2d, mod, gain, w_in_bf, seq, tm):
    rows, d = x2d.shape
    n_cols = w_in_bf.shape[2]
    n_blocks = n_cols // LANES
    tiles_per_seq = seq // tm
    return pl.pallas_call(
        _inproj_kernel,
        grid=(rows // tm,),
        in_specs=[
            pl.BlockSpec((tm, d), lambda i: (i, 0)),
            pl.BlockSpec((None, None, N_MOD, d), lambda i: (layer, i // tiles_per_seq, 0, 0)),
            pl.BlockSpec((None, 1, d), lambda i: (layer, 0, 0)),
            pl.BlockSpec((None, d, n_cols), lambda i: (layer, 0, 0),
                         pipeline_mode=pl.Buffered(1)),
        ],
        out_specs=pl.BlockSpec((n_blocks, tm, LANES), lambda i: (0, i, 0)),
        out_shape=jax.ShapeDtypeStruct((n_blocks, rows, LANES), F32),
        compiler_params=pltpu.CompilerParams(
            dimension_semantics=("arbitrary",), vmem_limit_bytes=VMEM_LIMIT_BYTES),
        name="inproj",
    )(x2d, mod, gain, w_in_bf)


def _hgrn_tables():
    c = HG_CHUNK
    t = np.arange(c)[:, None]
    s = np.arange(c)[None, :]
    tri = np.tile(np.stack([s <= t, s >= t]).astype(np.float32), (1, 1, 3))
    fwd = []
    for half in HG_LEVEL_HALVES:
        same = (t // (2 * half)) == (s // (2 * half))
        t_low = (t // half) % 2 == 1
        s_up = (s // half) % 2 == 0
        fwd.append(same & t_low & s_up)
    fwd.append(t == s)
    fwd = np.stack(fwd).astype(np.float32)
    masks = np.stack([fwd, fwd.transpose(0, 2, 1)])
    return tri, np.tile(masks, (1, 1, 1, 2))


def _hgrn_kernel(q_ref, ffw_ref, fbw_ref, i_ref, g_ref, lb_ref, gain_ref, tri_ref, mask_ref,
                 y_ref, st_scr, o_scr, qs_scr, k_scr, b_scr, v_scr, vt_scr):
    seq = q_ref.shape[0]
    n_chunks = seq // HG_CHUNK
    c = HG_CHUNK
    grp = HG_GROUP
    slab = min(HG_PREP_ROWS, seq)
    n_sub = 2 if seq % (2 * slab) == 0 else 1
    lb = lb_ref[...]
    one_m_lb = 1.0 - lb
    row = lax.broadcasted_iota(jnp.int32, (c, LANES), 0)
    left = lax.broadcasted_iota(jnp.int32, (c, LANES), 1) < c
    sub = lax.broadcasted_iota(jnp.int32, (8, LANES), 0)

    st_scr[...] = jnp.zeros(st_scr.shape, F32)

    def prep(i, carry):
        work = []
        for j in range(n_sub):
            r0 = pl.multiple_of((i * n_sub + j) * slab, slab)
            rows = pl.ds(r0, slab)
            zq = q_ref[rows, :]
            qs_scr[rows, :] = zq * _sigmoid(zq)
            v = i_ref[rows, :]
            v_scr[rows, :] = v.astype(BF16)
            for jc in range(slab // c):
                vt_scr[(i * n_sub + j) * (slab // c) + jc] = v[jc * c:(jc + 1) * c].T.astype(BF16)
            for d, f_ref in enumerate((ffw_ref, fbw_ref)):
                zf = f_ref[rows, :]
                t = jnp.exp(-jnp.abs(zf))
                neg = zf < 0
                inv = 1.0 / (1.0 + t)
                num = jnp.where(neg, lb + t, lb * t + 1.0)
                g = jnp.where(num > 0, jnp.log(num * inv), zf) * LOG2E
                k_scr[d, rows, :] = one_m_lb * (jnp.where(neg, 1.0, t) * inv)
                work.append((d, r0, _split3(g)))
        for d, r0, parts in work:
            for jc in range(slab // c):
                g3 = jnp.concatenate([p[jc * c:(jc + 1) * c] for p in parts], axis=0)
                b_scr[d, pl.ds(pl.multiple_of(r0 + jc * c, c), c), :] = _dot(tri_ref[d], g3)
        return carry

    lax.fori_loop(0, seq // (slab * n_sub), prep, 0)

    def mid_reference(d, r0, b, half):
        def brow(r, n):
            return jnp.broadcast_to(b_scr[d, pl.ds(r0 + r, 1), :], (n, LANES))

        if half == c // 2:
            return brow(half, c)
        if half >= 4:
            return jnp.concatenate(
                [brow(base + half, 2 * half) for base in range(0, c, 2 * half)], axis=0)
        assert half == 2
        return jnp.concatenate(
            [jnp.where(sub < 4, brow(base + 2, 8), brow(base + 6, 8))
             for base in range(0, c, 8)], axis=0)

    def trip(i):
        chunk_ids = [i * grp + j for j in range(grp)]
        chunk_ids += [n_chunks - 1 - (i * grp + j) for j in range(grp)]
        insts = [(slot // grp, pl.multiple_of(cid * c, c)) for slot, cid in enumerate(chunk_ids)]
        pairs = [(d, (s, s + 1)) for d in range(2) for s in range(d * grp, (d + 1) * grp, 2)]

        qs_l = [qs_scr[pl.ds(r0, c), :] for _, r0 in insts]
        k_l = [k_scr[d, pl.ds(r0, c), :] for d, r0 in insts]
        v_l = [v_scr[pl.ds(r0, c), :] for _, r0 in insts]
        b_l = [b_scr[d, pl.ds(r0, c), :] for d, r0 in insts]

        o_l = [None] * len(insts)
        for d in range(2):
            st = st_scr[d]
            for slot in range(d * grp, (d + 1) * grp):
                edge = c - 1 if d == 0 else 0
                b_edge = b_scr[d, pl.ds(insts[slot][1] + edge, 1), :]
                qe = (qs_l[slot] * jnp.exp2(b_l[slot])).astype(BF16)
                kd = (k_l[slot] * jnp.exp2(b_edge - b_l[slot])).astype(BF16)
                o_l[slot] = _dot(qe, st.T.astype(BF16))
                st = st * jnp.exp2(b_edge) + _dot(vt_scr[chunk_ids[slot]], kd)
            st_scr[d] = st

        zero = jnp.zeros((c, LANES), BF16)

        def block_diag(xa, xb):
            return jnp.concatenate([jnp.concatenate([xa, zero], axis=1),
                                    jnp.concatenate([zero, xb], axis=1)], axis=0)

        def pair_scores(xa_q, xa_k, xb_q, xb_k):
            return _dot_nt(jnp.concatenate([xa_q, xb_q], axis=1), block_diag(xa_k, xb_k))

        n_lvl = len(HG_LEVEL_HALVES)
        diag_l, adj_l = [], []
        for slot, (d, _) in enumerate(insts):
            qs, k, b = qs_l[slot], k_l[slot], b_l[slot]
            shift = 1 if d == 0 else c - 1
            k_adj = pltpu.roll(k, shift, axis=0)
            decay = jnp.exp2(jnp.minimum(b - pltpu.roll(b, shift, axis=0), 0.0))
            diag_l.append(jnp.sum(qs * k, axis=-1, keepdims=True))
            adj_l.append(jnp.sum(qs * k_adj * decay, axis=-1, keepdims=True))
        sc_l = []
        for d, (sa, sb) in pairs:
            sc_l.append(jnp.where(left, diag_l[sa], diag_l[sb]) * mask_ref[d, n_lvl]
                        + jnp.where(left, adj_l[sa], adj_l[sb]) * mask_ref[d, n_lvl - 1])
        for lvl, half in enumerate(HG_LEVEL_HALVES[:-1]):
            for p, (d, slots) in enumerate(pairs):
                xs = []
                for slot in slots:
                    b = b_l[slot]
                    q_rows = ((row & half) != 0) if d == 0 else ((row & half) == 0)
                    sign = jnp.where(q_rows, 1.0, -1.0)
                    e = jnp.exp2((b - mid_reference(d, insts[slot][1], b, half)) * sign)
                    xs.append((jnp.where(q_rows, qs_l[slot], k_l[slot]) * e).astype(BF16))
                sc_l[p] = sc_l[p] + pair_scores(xs[0], xs[0], xs[1], xs[1]) * mask_ref[d, lvl]

        for p, (d, (sa, sb)) in enumerate(pairs):
            o_intra = _dot(sc_l[p].astype(BF16),
                           block_diag(v_l[sa], v_l[sb]))
            o_scr[d, pl.ds(insts[sa][1], c), :] = o_l[sa] + o_intra[:, :LANES]
            o_scr[d, pl.ds(insts[sb][1], c), :] = o_l[sb] + o_intra[:, LANES:]

    def body(i, carry):
        trip(i)
        return carry

    lax.fori_loop(0, n_chunks // grp, body, 0)

    gain = gain_ref[...]

    def fin(i, carry):
        for j in range(n_sub):
            rows = pl.ds(pl.multiple_of((i * n_sub + j) * slab, slab), slab)
            o = o_scr[0, rows, :] + o_scr[1, rows, :]
            zg = g_ref[rows, :]
            y = _row_rms(o, gain) * (zg * _sigmoid(zg))
            y_ref[rows, :] = y.astype(y_ref.dtype)
        return carry

    lax.fori_loop(0, seq // (slab * n_sub), fin, 0)


def _hgrn_call(layer, z4, lb, out_gain):
    _, nb, seq, _ = z4.shape
    tri, masks = _hgrn_tables()

    def zspec(cb0):
        return pl.BlockSpec((None, None, seq, LANES), lambda b, h: (cb0 + h, b, 0, 0))

    y = pl.pallas_call(
        _hgrn_kernel,
        grid=(nb, HG_HEADS),
        in_specs=[
            zspec(CB_HQ), zspec(CB_FFW), zspec(CB_FBW), zspec(CB_HI), zspec(CB_HG),
            pl.BlockSpec((None, None, 1, LANES), lambda b, h: (layer, h, 0, 0)),
            pl.BlockSpec((None, 1, LANES), lambda b, h: (layer, 0, 0)),
            pl.BlockSpec(tri.shape, lambda b, h: (0, 0, 0)),
            pl.BlockSpec(masks.shape, lambda b, h: (0, 0, 0, 0)),
        ],
        out_specs=pl.BlockSpec((None, None, seq, LANES), lambda b, h: (h, b, 0, 0)),
        out_shape=jax.ShapeDtypeStruct((HG_HEADS, nb, seq, LANES), BF16),
        scratch_shapes=[
            pltpu.VMEM((2, HG_HEAD_DIM, HG_HEAD_DIM), F32),
            pltpu.VMEM((2, seq, LANES), F32),
            pltpu.VMEM((seq, LANES), F32),
            pltpu.VMEM((2, seq, LANES), F32),
            pltpu.VMEM((2, seq, LANES), F32),
            pltpu.VMEM((seq, LANES), BF16),
            pltpu.VMEM((seq // HG_CHUNK, HG_HEAD_DIM, HG_CHUNK), BF16),
        ],
        compiler_params=pltpu.CompilerParams(
            dimension_semantics=("arbitrary", "arbitrary"), vmem_limit_bytes=VMEM_LIMIT_BYTES),
        name="hgrn",
    )(z4, z4, z4, z4, z4, lb, out_gain, jnp.asarray(tri, BF16), jnp.asarray(masks, F32))
    return y.reshape(HG_HEADS, nb * seq, LANES)


def _rope_tables(seq):
    pos = jnp.arange(seq)
    row = (pos // GRID_W).astype(F32)
    col = (pos % GRID_W).astype(F32)
    half = ATT_HEAD_DIM // 2
    inv_freq = 1.0 / (ROPE_THETA ** (jnp.arange(0, half, 2, dtype=F32) / half))
    ang_r = row[:, None] * inv_freq
    ang_c = col[:, None] * inv_freq
    cos = jnp.concatenate([jnp.cos(ang_r)] * 2 + [jnp.cos(ang_c)] * 2, axis=-1)
    sin = jnp.concatenate([-jnp.sin(ang_r), jnp.sin(ang_r), -jnp.sin(ang_c), jnp.sin(ang_c)], axis=-1)
    reps = LANES // ATT_HEAD_DIM
    return jnp.tile(cos, (1, reps)), jnp.tile(sin, (1, reps))


def _rope(x, cos, sin, lane):
    quarter = ATT_HEAD_DIM // 4
    fwd = pltpu.roll(x, LANES - quarter, axis=1)
    bwd = pltpu.roll(x, quarter, axis=1)
    partner = jnp.where((lane & quarter) == 0, fwd, bwd)
    return x * cos + partner * sin


def _attn_kernel(q0_ref, q1_ref, q2_ref, q3_ref, k_ref, v_ref, cosq_ref, sinq_ref, cosk_ref, sink_ref,
                 qg_ref, kg_ref, og_ref, gmat_ref, y_ref, kext_scr, vext_scr):
    seq = k_ref.shape[0]
    tq = q0_ref.shape[0]
    gmat = gmat_ref[...]

    @pl.when(pl.program_id(1) == 0)
    def _prepare_kv():
        slab = min(512, seq)
        lane = lax.broadcasted_iota(jnp.int32, (slab, LANES), 1)
        first = lane < ATT_HEAD_DIM
        for r0 in range(0, seq, slab):
            rs = slice(r0, r0 + slab)
            k = k_ref[rs, :]
            kn = k * lax.rsqrt(_group_mean_sq(k, gmat) + EPS) * kg_ref[...]
            kr = _rope(kn, cosk_ref[rs, :], sink_ref[rs, :], lane)
            kr_sw = pltpu.roll(kr, ATT_HEAD_DIM, axis=1)
            kext_scr[0, rs, :] = jnp.where(first, kr, kr_sw).astype(BF16)
            kext_scr[1, rs, :] = jnp.where(first, kr_sw, kr).astype(BF16)
            v = v_ref[rs, :]
            v_sw = pltpu.roll(v, ATT_HEAD_DIM, axis=1)
            vext_scr[0, 0, rs, :] = jnp.where(first, v, 1.0).astype(BF16)
            vext_scr[0, 1, rs, :] = jnp.where(first, 1.0, v_sw).astype(BF16)
            vext_scr[1, 0, rs, :] = jnp.where(first, v_sw, 1.0).astype(BF16)
            vext_scr[1, 1, rs, :] = jnp.where(first, 1.0, v).astype(BF16)

    lane = lax.broadcasted_iota(jnp.int32, (tq, LANES), 1)
    first = lane < ATT_HEAD_DIM
    cosq = cosq_ref[...]
    sinq = sinq_ref[...]
    sm_scale = ATT_HEAD_DIM ** -0.5 * LOG2E
    q_refs = (q0_ref, q1_ref, q2_ref, q3_ref)
    heads = [(pair, parity) for pair in range(ATT_PAIRS) for parity in range(2)]

    def scores(pair, parity):
        if parity == 0:
            q = q_refs[pair][...]
            qn = q * lax.rsqrt(_group_mean_sq(q, gmat) + EPS) * (qg_ref[...] * sm_scale)
            scores.qr = _rope(qn, cosq, sinq, lane)
        keep = first if parity == 0 else jnp.logical_not(first)
        qm = jnp.where(keep, scores.qr, 0.0).astype(BF16)
        return _dot_nt(qm, kext_scr[pair // PAIRS_PER_KV])

    halves = []

    def values(pair, parity, p):
        ov = _dot(p, vext_scr[pair // PAIRS_PER_KV, parity])
        halves.append(ov / pltpu.roll(ov, ATT_HEAD_DIM, axis=1))
        if parity == 1:
            o = jnp.where(first, halves[0], halves[1])
            y = o * lax.rsqrt(_group_mean_sq(o, gmat) + EPS) * og_ref[...]
            y_ref[pair] = y.astype(y_ref.dtype)
            halves.clear()

    s = scores(*heads[0])
    pending = None
    for idx, head in enumerate(heads):
        s_next = scores(*heads[idx + 1]) if idx + 1 < len(heads) else None
        if pending is not None:
            values(*pending)
        m = jnp.max(s, axis=-1, keepdims=True)
        pending = (*head, jnp.exp2(s - m).astype(BF16))
        s = s_next
    values(*pending)


def _attn_call(layer, z4, cos, sin, q_gain, k_gain, out_gain, tq):
    _, nb, seq, _ = z4.shape
    nq = seq // tq
    lanes2 = np.arange(2 * LANES)
    grp = (lanes2[:, None] // ATT_HEAD_DIM) == (lanes2[None, :] // ATT_HEAD_DIM)
    gmat = jnp.asarray(grp.astype(np.float32) / ATT_HEAD_DIM, BF16)

    def qspec(pair):
        return pl.BlockSpec((None, None, tq, LANES), lambda b, i: (CB_AQ + pair, b, i, 0))

    def full(cb):
        return pl.BlockSpec((None, None, seq, LANES), lambda b, i: (cb, b, 0, 0))

    vec = pl.BlockSpec((None, 1, LANES), lambda b, i: (layer, 0, 0))
    return pl.pallas_call(
        _attn_kernel,
        grid=(nb, nq),
        in_specs=[
            qspec(0), qspec(1), qspec(2), qspec(3), full(CB_AK), full(CB_AV),
            pl.BlockSpec((tq, LANES), lambda b, i: (i, 0)),
            pl.BlockSpec((tq, LANES), lambda b, i: (i, 0)),
            pl.BlockSpec((seq, LANES), lambda b, i: (0, 0)),
            pl.BlockSpec((seq, LANES), lambda b, i: (0, 0)),
            vec, vec, vec,
            pl.BlockSpec((2 * LANES, 2 * LANES), lambda b, i: (0, 0)),
        ],
        out_specs=pl.BlockSpec((ATT_PAIRS, tq, LANES), lambda b, i: (0, b * nq + i, 0)),
        out_shape=jax.ShapeDtypeStruct((ATT_PAIRS, nb * seq, LANES), BF16),
        scratch_shapes=[
            pltpu.VMEM((ATT_KV_HEADS, seq, LANES), BF16),
            pltpu.VMEM((ATT_KV_HEADS, 2, seq, LANES), BF16),
        ],
        compiler_params=pltpu.CompilerParams(
            dimension_semantics=("arbitrary", "arbitrary"), vmem_limit_bytes=VMEM_LIMIT_BYTES),
        name="attn",
    )(z4, z4, z4, z4, z4, z4, cos, sin, cos, sin, q_gain, k_gain, out_gain, gmat)


def _outffn_kernel(ah_ref, aa_ref, x_ref, mod_ref, g_post_mix_ref, g_pre_ff_ref, g_post_ff_ref,
                   wo_ref, w1_ref, w2_ref, o_ref):
    gate1 = mod_ref[2:3, :]
    shift2 = mod_ref[3:4, :]
    scale2 = mod_ref[4:5, :]
    gate2 = mod_ref[5:6, :]
    d_ff = w1_ref.shape[1]
    fc = 1024
    part_rows = min(ROW_PART, x_ref.shape[0])
    parts = [pl.ds(r0, part_rows) for r0 in range(0, x_ref.shape[0], part_rows)]
    ys = []
    for rows in parts:
        a = jnp.concatenate([ah_ref[j, rows, :] for j in range(ah_ref.shape[0])]
                            + [aa_ref[j, rows, :] for j in range(aa_ref.shape[0])], axis=1)
        ys.append(_dot(a, wo_ref[...]))
    for rows, y in zip(parts, ys):
        x1 = x_ref[rows, :] + gate1 * _row_rms(y, g_post_mix_ref[...])
        h = (_row_rms(x1, g_pre_ff_ref[...]) * (1.0 + scale2) + shift2).astype(BF16)
        acc = None
        for c0 in range(0, d_ff, fc):
            u = jnp.maximum(_dot(h, w1_ref[:, c0:c0 + fc]), 0.0)
            part = _dot((u * u).astype(BF16), w2_ref[c0:c0 + fc, :])
            acc = part if acc is None else acc + part
        o_ref[rows, :] = x1 + gate2 * _row_rms(acc, g_post_ff_ref[...])


def _outffn_call(layer, ah, aa, x2d, mod, g_post_mix, g_pre_ff, g_post_ff, wo_bf, w1_bf, w2_bf,
                 seq, tm):
    rows, d = x2d.shape
    d_ff = w1_bf.shape[2]
    tiles_per_seq = seq // tm
    vec = pl.BlockSpec((None, 1, d), lambda i: (layer, 0, 0))

    def resident(shape):
        return pl.BlockSpec((None,) + shape, lambda i: (layer, 0, 0),
                            pipeline_mode=pl.Buffered(1))

    return pl.pallas_call(
        _outffn_kernel,
        grid=(rows // tm,),
        in_specs=[
            pl.BlockSpec((ah.shape[0], tm, LANES), lambda i: (0, i, 0)),
            pl.BlockSpec((aa.shape[0], tm, LANES), lambda i: (0, i, 0)),
            pl.BlockSpec((tm, d), lambda i: (i, 0)),
            pl.BlockSpec((None, None, N_MOD, d), lambda i: (layer, i // tiles_per_seq, 0, 0)),
            vec, vec, vec,
            resident((d, d)), resident((d, d_ff)), resident((d_ff, d)),
        ],
        out_specs=pl.BlockSpec((tm, d), lambda i: (i, 0)),
        out_shape=jax.ShapeDtypeStruct((rows, d), F32),
        compiler_params=pltpu.CompilerParams(
            dimension_semantics=("arbitrary",), vmem_limit_bytes=VMEM_LIMIT_BYTES),
        name="outffn",
    )(ah, aa, x2d, mod, g_post_mix, g_pre_ff, g_post_ff, wo_bf, w1_bf, w2_bf)


def kernel(x, c, w_ada, ada_layer_bias, hg_lb_logits, pre_mix, post_mix, w_in, w_out, hg_out_gain,
           q_gain, k_gain, att_out_gain, pre_ff, post_ff, w_ff1, w_ff2):
    nb, seq, d = x.shape
    depth = w_in.shape[0]
    assert w_in.shape[2] == N_COL_BLOCKS * LANES and d == HG_WIDTH + ATT_WIDTH
    tm = min(1024, seq)
    tq = min(256, seq)

    mod, lb_table = _cond_call(c, w_ada, ada_layer_bias, hg_lb_logits)
    mod = mod.reshape(depth, nb, N_MOD, d)
    lb_table = lb_table.reshape(depth, HG_HEADS, 1, LANES)
    cos, sin = _rope_tables(seq)

    def rows3(a):
        return a.reshape(depth, 1, a.shape[-1])

    def head_pair(g):
        return rows3(jnp.tile(g, (1, LANES // ATT_HEAD_DIM)))

    w_in_bf, w_out_bf = w_in.astype(BF16), w_out.astype(BF16)
    w_ff1_bf, w_ff2_bf = w_ff1.astype(BF16), w_ff2.astype(BF16)
    pre_mix, post_mix, pre_ff, post_ff = map(rows3, (pre_mix, post_mix, pre_ff, post_ff))
    hg_out_gain = rows3(hg_out_gain)
    q_gain, k_gain, att_out_gain = map(head_pair, (q_gain, k_gain, att_out_gain))

    x2d = x.reshape(nb * seq, d)
    for l in range(depth):
        z = _inproj_call(l, x2d, mod, pre_mix, w_in_bf, seq, tm)
        z4 = z.reshape(N_COL_BLOCKS, nb, seq, LANES)
        y_hgrn = _hgrn_call(l, z4, lb_table, hg_out_gain)
        y_att = _attn_call(l, z4, cos, sin, q_gain, k_gain, att_out_gain, tq)
        x2d = _outffn_call(l, y_hgrn, y_att, x2d, mod, post_mix, pre_ff, post_ff,
                           w_out_bf, w_ff1_bf, w_ff2_bf, seq, tm)
    return x2d.reshape(nb, seq, d)
```

```python
import functools

import numpy as np
import jax
import jax.numpy as jnp
from jax import lax
from jax.experimental import pallas as pl
from jax.experimental.pallas import tpu as pltpu

F32 = jnp.float32
BF16 = jnp.bfloat16

LANES = 128
VMEM_LIMIT_BYTES = 56 * 1024 * 1024
ROW_PART = 256

EPS = 1e-6
N_MOD = 6
HG_HEADS = 4
HG_HEAD_DIM = 128
HG_WIDTH = HG_HEADS * HG_HEAD_DIM
HG_CHUNK = 64
HG_GROUP = 8
HG_PREP_ROWS = 256
LOG2E = 1.4426950408889634
HG_LEVEL_HALVES = (32, 16, 8, 4, 2, 1)
ATT_HEADS = 8
ATT_HEAD_DIM = 64
ATT_KV_HEADS = 2
ATT_WIDTH = ATT_HEADS * ATT_HEAD_DIM
KV_WIDTH = ATT_KV_HEADS * ATT_HEAD_DIM
GRID_W = 64
ROPE_THETA = 10000.0

CB_HQ, CB_FFW, CB_FBW, CB_HI, CB_HG = 0, 4, 8, 12, 16
CB_AQ, CB_AK, CB_AV = 20, 24, 25
N_COL_BLOCKS = 26
OB_QS, OB_K, OB_B, OB_V, OB_SG = 0, 4, 12, 20, 24
OB_AQ, OB_AK, OB_AV = 28, 32, 33
N_OUT_BLOCKS = 34
ATT_PAIRS = ATT_WIDTH // LANES
PAIRS_PER_KV = ATT_PAIRS // ATT_KV_HEADS


def _dot(a, b):
    return jnp.dot(a, b, preferred_element_type=F32)


def _dot_nt(a, b):
    return lax.dot_general(a, b, (((1,), (1,)), ((), ())), preferred_element_type=F32)


def _split2(x):
    hi = x.astype(BF16)
    lo = (x - hi.astype(F32)).astype(BF16)
    return hi, lo


def _split3(x):
    hi = x.astype(BF16)
    r1 = x - hi.astype(F32)
    mid = r1.astype(BF16)
    lo = (r1 - mid.astype(F32)).astype(BF16)
    return hi, mid, lo


def _sigmoid(x):
    t = jnp.exp(-jnp.abs(x))
    return jnp.where(x >= 0, 1.0, t) / (1.0 + t)


def _row_rms(x, gain):
    ms = jnp.mean(x * x, axis=-1, keepdims=True)
    return x * lax.rsqrt(ms + EPS) * gain


def _group_mean_sq(x, gmat):
    w = x.shape[1]
    r = _dot(jnp.concatenate(_split2(x * x), axis=1), gmat)
    return r[:, :w] + r[:, w:]


def _cond_kernel(c_ref, w_ref, bias_ref, lbl_ref, mod_ref, lb_ref):
    depth = bias_ref.shape[0]
    c = c_ref[...]
    s = c * _sigmoid(c)
    s_hi = s.astype(BF16).astype(F32)
    s_lo = s - s_hi
    w_hi, w_lo = _split2(w_ref[...])
    nb = c.shape[0]
    r1 = _dot(jnp.concatenate([s_hi, s_lo], axis=0).astype(BF16), w_hi)
    r2 = _dot(jnp.concatenate([s_hi, s_hi], axis=0).astype(BF16), w_lo)
    cond = r1[:nb] + r1[nb:] + r2[:nb]
    for l in range(depth):
        mod_ref[l] = cond + bias_ref[l:l + 1, :]

    rows = [lbl_ref[l:l + 1, :] for l in range(depth)]
    m = functools.reduce(jnp.maximum, rows)
    es = [jnp.exp(r - m) for r in rows]
    tot = functools.reduce(lambda a, b: a + b, es)
    cum = None
    first = None
    for l in range(depth):
        p = es[l] / tot
        cum = p if cum is None else cum + p
        if first is None:
            first = cum
        lb_ref[l:l + 1, :] = cum - first


def _cond_call(c, w_ada, ada_layer_bias, hg_lb_logits):
    nb, d = c.shape
    n_out = w_ada.shape[1]
    depth = ada_layer_bias.shape[0]
    tn = n_out // 4
    return pl.pallas_call(
        _cond_kernel,
        grid=(n_out // tn,),
        in_specs=[
            pl.BlockSpec((nb, d), lambda j: (0, 0)),
            pl.BlockSpec((d, tn), lambda j: (0, j)),
            pl.BlockSpec((depth, tn), lambda j: (0, j)),
            pl.BlockSpec(hg_lb_logits.shape, lambda j: (0, 0)),
        ],
        out_specs=[
            pl.BlockSpec((depth, nb, tn), lambda j: (0, 0, j)),
            pl.BlockSpec(hg_lb_logits.shape, lambda j: (0, 0)),
        ],
        out_shape=[
            jax.ShapeDtypeStruct((depth, nb, n_out), F32),
            jax.ShapeDtypeStruct(hg_lb_logits.shape, F32),
        ],
        compiler_params=pltpu.CompilerParams(
            dimension_semantics=("arbitrary",), vmem_limit_bytes=VMEM_LIMIT_BYTES),
        name="cond",
    )(c, w_ada, ada_layer_bias, hg_lb_logits)


def _silu(z):
    return z / (1.0 + jnp.exp2(z * -LOG2E))


def _hgrn_gate(zf, lb, one_m_lb):
    t = jnp.exp(-jnp.abs(zf))
    neg = zf < 0
    inv = 1.0 / (1.0 + t)
    num = jnp.where(neg, lb + t, lb * t + 1.0)
    g = jnp.where(num > 0, jnp.log(num * inv), zf) * LOG2E
    return one_m_lb * (jnp.where(neg, 1.0, t) * inv), g


def _inproj_kernel(x_ref, mod_ref, gain_ref, w_ref, lb_ref, tri_ref, z_ref):
    shift = mod_ref[0:1, :]
    scale = mod_ref[1:2, :]
    c = HG_CHUNK

    def emit(cb, rows, r0, z, decays):
        if cb >= CB_AQ:
            z_ref[OB_AQ + cb - CB_AQ, rows, :] = z
        elif cb >= CB_HG:
            z_ref[OB_SG + cb - CB_HG, rows, :] = _silu(z)
        elif cb >= CB_HI:
            z_ref[OB_V + cb - CB_HI, rows, :] = z
        elif cb >= CB_FFW:
            d, head = divmod(cb - CB_FFW, HG_HEADS)
            lb = lb_ref[head]
            k, g = _hgrn_gate(z, lb, 1.0 - lb)
            z_ref[OB_K + d * HG_HEADS + head, rows, :] = k
            decays.append((d, head, r0, _split3(g)))
        else:
            z_ref[OB_QS + cb - CB_HQ, rows, :] = _silu(z)

    def cumulate(decays):
        for d, head, r0, parts in decays:
            for jc in range(parts[0].shape[0] // c):
                g3 = jnp.concatenate([p[jc * c:(jc + 1) * c] for p in parts], axis=0)
                z_ref[OB_B + d * HG_HEADS + head, pl.ds(r0 + jc * c, c), :] = _dot(tri_ref[d], g3)

    part_rows = min(ROW_PART, x_ref.shape[0])
    pending = []
    for r0 in range(0, x_ref.shape[0], part_rows):
        rows = pl.ds(r0, part_rows)
        h = _row_rms(x_ref[rows, :], gain_ref[...]) * (1.0 + scale) + shift
        hb = h.astype(BF16)
        decays = []
        for cb in range(0, N_COL_BLOCKS, 2):
            z = _dot(hb, w_ref[:, cb * LANES:(cb + 2) * LANES])
            emit(cb, rows, r0, z[:, :LANES], decays)
            emit(cb + 1, rows, r0, z[:, LANES:], decays)
        cumulate(pending)
        pending = decays
    cumulate(pending)


def _inproj_call(layer, x2d, mod, gain, w_in_bf, lb, seq, tm):
    rows, d = x2d.shape
    n_cols = w_in_bf.shape[2]
    n_blocks = N_OUT_BLOCKS
    tiles_per_seq = seq // tm
    tri = _hgrn_tri()
    return pl.pallas_call(
        _inproj_kernel,
        grid=(rows // tm,),
        in_specs=[
            pl.BlockSpec((tm, d), lambda i: (i, 0)),
            pl.BlockSpec((None, None, N_MOD, d), lambda i: (layer, i // tiles_per_seq, 0, 0)),
            pl.BlockSpec((None, 1, d), lambda i: (layer, 0, 0)),
            pl.BlockSpec((None, d, n_cols), lambda i: (layer, 0, 0),
                         pipeline_mode=pl.Buffered(1)),
            pl.BlockSpec((None, HG_HEADS, 1, LANES), lambda i: (layer, 0, 0, 0)),
            pl.BlockSpec(tri.shape, lambda i: (0, 0, 0)),
        ],
        out_specs=pl.BlockSpec((n_blocks, tm, LANES), lambda i: (0, i, 0)),
        out_shape=jax.ShapeDtypeStruct((n_blocks, rows, LANES), F32),
        compiler_params=pltpu.CompilerParams(
            dimension_semantics=("arbitrary",), vmem_limit_bytes=VMEM_LIMIT_BYTES),
        name="inproj",
    )(x2d, mod, gain, w_in_bf, lb, jnp.asarray(tri, BF16))


def _hgrn_tri():
    t = np.arange(HG_CHUNK)[:, None]
    s = np.arange(HG_CHUNK)[None, :]
    return np.tile(np.stack([s <= t, s >= t]).astype(np.float32), (1, 1, 3))


def _hgrn_masks():
    c = HG_CHUNK
    t = np.arange(c)[:, None]
    s = np.arange(c)[None, :]
    fwd = []
    for half in HG_LEVEL_HALVES:
        same = (t // (2 * half)) == (s // (2 * half))
        t_low = (t // half) % 2 == 1
        s_up = (s // half) % 2 == 0
        fwd.append(same & t_low & s_up)
    fwd.append(t == s)
    fwd = np.stack(fwd).astype(np.float32)
    masks = np.stack([fwd, fwd.transpose(0, 2, 1)])
    return np.tile(masks, (1, 1, 1, 2))


def _hgrn_kernel(qs_ref, kfw_ref, bfw_ref, kbw_ref, bbw_ref, v_ref, sg_ref, gain_ref, mask_ref,
                 y_ref, st_scr, o_scr, v_scr, vt_scr):
    seq = qs_ref.shape[0]
    n_chunks = seq // HG_CHUNK
    c = HG_CHUNK
    grp = HG_GROUP
    slab = min(HG_PREP_ROWS, seq)
    n_sub = 2 if seq % (2 * slab) == 0 else 1
    qs_scr = qs_ref
    k_scr = (kfw_ref, kbw_ref)
    b_scr = (bfw_ref, bbw_ref)
    row = lax.broadcasted_iota(jnp.int32, (c, LANES), 0)
    left = lax.broadcasted_iota(jnp.int32, (c, LANES), 1) < c
    sub = lax.broadcasted_iota(jnp.int32, (8, LANES), 0)

    st_scr[...] = jnp.zeros(st_scr.shape, F32)

    def prep(i, carry):
        for j in range(n_sub):
            rows = pl.ds(pl.multiple_of((i * n_sub + j) * slab, slab), slab)
            v = v_ref[rows, :]
            v_scr[rows, :] = v.astype(BF16)
            for jc in range(slab // c):
                vt_scr[(i * n_sub + j) * (slab // c) + jc] = v[jc * c:(jc + 1) * c].T.astype(BF16)
        return carry

    lax.fori_loop(0, seq // (slab * n_sub), prep, 0)

    def mid_reference(d, r0, b, half):
        def brow(r, n):
            return jnp.broadcast_to(b_scr[d][pl.ds(r0 + r, 1), :], (n, LANES))

        if half == c // 2:
            return brow(half, c)
        if half >= 4:
            return jnp.concatenate(
                [brow(base + half, 2 * half) for base in range(0, c, 2 * half)], axis=0)
        assert half == 2
        return jnp.concatenate(
            [jnp.where(sub < 4, brow(base + 2, 8), brow(base + 6, 8))
             for base in range(0, c, 8)], axis=0)

    def trip(i):
        chunk_ids = [i * grp + j for j in range(grp)]
        chunk_ids += [n_chunks - 1 - (i * grp + j) for j in range(grp)]
        insts = [(slot // grp, pl.multiple_of(cid * c, c)) for slot, cid in enumerate(chunk_ids)]
        pairs = [(d, (s, s + 1)) for d in range(2) for s in range(d * grp, (d + 1) * grp, 2)]

        qs_l = [qs_scr[pl.ds(r0, c), :] for _, r0 in insts]
        k_l = [k_scr[d][pl.ds(r0, c), :] for d, r0 in insts]
        v_l = [v_scr[pl.ds(r0, c), :] for _, r0 in insts]
        b_l = [b_scr[d][pl.ds(r0, c), :] for d, r0 in insts]

        o_l = [None] * len(insts)
        for d in range(2):
            st = st_scr[d]
            for slot in range(d * grp, (d + 1) * grp):
                edge = c - 1 if d == 0 else 0
                b_edge = b_scr[d][pl.ds(insts[slot][1] + edge, 1), :]
                qe = (qs_l[slot] * jnp.exp2(b_l[slot])).astype(BF16)
                kd = (k_l[slot] * jnp.exp2(b_edge - b_l[slot])).astype(BF16)
                o_l[slot] = _dot(qe, st.T.astype(BF16))
                st = st * jnp.exp2(b_edge) + _dot(vt_scr[chunk_ids[slot]], kd)
            st_scr[d] = st

        zero = jnp.zeros((c, LANES), BF16)

        def block_diag(xa, xb):
            return jnp.concatenate([jnp.concatenate([xa, zero], axis=1),
                                    jnp.concatenate([zero, xb], axis=1)], axis=0)

        def pair_scores(xa_q, xa_k, xb_q, xb_k):
            return _dot_nt(jnp.concatenate([xa_q, xb_q], axis=1), block_diag(xa_k, xb_k))

        n_lvl = len(HG_LEVEL_HALVES)
        diag_l, adj_l = [], []
        for slot, (d, _) in enumerate(insts):
            qs, k, b = qs_l[slot], k_l[slot], b_l[slot]
            shift = 1 if d == 0 else c - 1
            k_adj = pltpu.roll(k, shift, axis=0)
            decay = jnp.exp2(jnp.minimum(b - pltpu.roll(b, shift, axis=0), 0.0))
            diag_l.append(jnp.sum(qs * k, axis=-1, keepdims=True))
            adj_l.append(jnp.sum(qs * k_adj * decay, axis=-1, keepdims=True))
        sc_l = []
        for d, (sa, sb) in pairs:
            sc_l.append(jnp.where(left, diag_l[sa], diag_l[sb]) * mask_ref[d, n_lvl]
                        + jnp.where(left, adj_l[sa], adj_l[sb]) * mask_ref[d, n_lvl - 1])
        for lvl, half in enumerate(HG_LEVEL_HALVES[:-1]):
            for p, (d, slots) in enumerate(pairs):
                xs = []
                for slot in slots:
                    b = b_l[slot]
                    q_rows = ((row & half) != 0) if d == 0 else ((row & half) == 0)
                    sign = jnp.where(q_rows, 1.0, -1.0)
                    e = jnp.exp2((b - mid_reference(d, insts[slot][1], b, half)) * sign)
                    xs.append((jnp.where(q_rows, qs_l[slot], k_l[slot]) * e).astype(BF16))
                sc_l[p] = sc_l[p] + pair_scores(xs[0], xs[0], xs[1], xs[1]) * mask_ref[d, lvl]

        for p, (d, (sa, sb)) in enumerate(pairs):
            o_intra = _dot(sc_l[p].astype(BF16),
                           block_diag(v_l[sa], v_l[sb]))
            o_scr[d, pl.ds(insts[sa][1], c), :] = o_l[sa] + o_intra[:, :LANES]
            o_scr[d, pl.ds(insts[sb][1], c), :] = o_l[sb] + o_intra[:, LANES:]

    def body(i, carry):
        trip(i)
        return carry

    lax.fori_loop(0, n_chunks // grp, body, 0)

    gain = gain_ref[...]

    def fin(i, carry):
        for j in range(n_sub):
            rows = pl.ds(pl.multiple_of((i * n_sub + j) * slab, slab), slab)
            o = o_scr[0, rows, :] + o_scr[1, rows, :]
            y_ref[rows, :] = (_row_rms(o, gain) * sg_ref[rows, :]).astype(y_ref.dtype)
        return carry

    lax.fori_loop(0, seq // (slab * n_sub), fin, 0)


def _hgrn_call(layer, z4, out_gain):
    _, nb, seq, _ = z4.shape
    assert seq % (HG_GROUP * HG_CHUNK) == 0 and seq % HG_PREP_ROWS == 0
    masks = _hgrn_masks()

    def zspec(ob0):
        return pl.BlockSpec((None, None, seq, LANES), lambda b, h: (ob0 + h, b, 0, 0))

    y = pl.pallas_call(
        _hgrn_kernel,
        grid=(nb, HG_HEADS),
        in_specs=[
            zspec(OB_QS), zspec(OB_K), zspec(OB_B), zspec(OB_K + HG_HEADS), zspec(OB_B + HG_HEADS),
            zspec(OB_V), zspec(OB_SG),
            pl.BlockSpec((None, 1, LANES), lambda b, h: (layer, 0, 0)),
            pl.BlockSpec(masks.shape, lambda b, h: (0, 0, 0, 0)),
        ],
        out_specs=pl.BlockSpec((None, None, seq, LANES), lambda b, h: (h, b, 0, 0)),
        out_shape=jax.ShapeDtypeStruct((HG_HEADS, nb, seq, LANES), BF16),
        scratch_shapes=[
            pltpu.VMEM((2, HG_HEAD_DIM, HG_HEAD_DIM), F32),
            pltpu.VMEM((2, seq, LANES), F32),
            pltpu.VMEM((seq, LANES), BF16),
            pltpu.VMEM((seq // HG_CHUNK, HG_HEAD_DIM, HG_CHUNK), BF16),
        ],
        compiler_params=pltpu.CompilerParams(
            dimension_semantics=("arbitrary", "arbitrary"), vmem_limit_bytes=VMEM_LIMIT_BYTES),
        name="hgrn",
    )(z4, z4, z4, z4, z4, z4, z4, out_gain, jnp.asarray(masks, F32))
    return y.reshape(HG_HEADS, nb * seq, LANES)


def _rope_tables(seq):
    pos = jnp.arange(seq)
    row = (pos // GRID_W).astype(F32)
    col = (pos % GRID_W).astype(F32)
    half = ATT_HEAD_DIM // 2
    inv_freq = 1.0 / (ROPE_THETA ** (jnp.arange(0, half, 2, dtype=F32) / half))
    ang_r = row[:, None] * inv_freq
    ang_c = col[:, None] * inv_freq
    cos = jnp.concatenate([jnp.cos(ang_r)] * 2 + [jnp.cos(ang_c)] * 2, axis=-1)
    sin = jnp.concatenate([-jnp.sin(ang_r), jnp.sin(ang_r), -jnp.sin(ang_c), jnp.sin(ang_c)], axis=-1)
    reps = LANES // ATT_HEAD_DIM
    return jnp.tile(cos, (1, reps)), jnp.tile(sin, (1, reps))


def _rope(x, cos, sin, lane):
    quarter = ATT_HEAD_DIM // 4
    fwd = pltpu.roll(x, LANES - quarter, axis=1)
    bwd = pltpu.roll(x, quarter, axis=1)
    partner = jnp.where((lane & quarter) == 0, fwd, bwd)
    return x * cos + partner * sin


def _attn_kernel(q0_ref, q1_ref, q2_ref, q3_ref, k_ref, v_ref, cosq_ref, sinq_ref, cosk_ref, sink_ref,
                 qg_ref, kg_ref, og_ref, gmat_ref, y_ref, kext_scr, vext_scr):
    seq = k_ref.shape[0]
    tq = q0_ref.shape[0]
    gmat = gmat_ref[...]

    @pl.when(pl.program_id(1) == 0)
    def _prepare_kv():
        slab = min(512, seq)
        lane = lax.broadcasted_iota(jnp.int32, (slab, LANES), 1)
        first = lane < ATT_HEAD_DIM
        for r0 in range(0, seq, slab):
            rs = slice(r0, r0 + slab)
            k = k_ref[rs, :]
            kn = k * lax.rsqrt(_group_mean_sq(k, gmat) + EPS) * kg_ref[...]
            kr = _rope(kn, cosk_ref[rs, :], sink_ref[rs, :], lane)
            kr_sw = pltpu.roll(kr, ATT_HEAD_DIM, axis=1)
            kext_scr[0, rs, :] = jnp.where(first, kr, kr_sw).astype(BF16)
            kext_scr[1, rs, :] = jnp.where(first, kr_sw, kr).astype(BF16)
            v = v_ref[rs, :]
            v_sw = pltpu.roll(v, ATT_HEAD_DIM, axis=1)
            vext_scr[0, 0, rs, :] = jnp.where(first, v, 1.0).astype(BF16)
            vext_scr[0, 1, rs, :] = jnp.where(first, 1.0, v_sw).astype(BF16)
            vext_scr[1, 0, rs, :] = jnp.where(first, v_sw, 1.0).astype(BF16)
            vext_scr[1, 1, rs, :] = jnp.where(first, 1.0, v).astype(BF16)

    lane = lax.broadcasted_iota(jnp.int32, (tq, LANES), 1)
    first = lane < ATT_HEAD_DIM
    cosq = cosq_ref[...]
    sinq = sinq_ref[...]
    sm_scale = ATT_HEAD_DIM ** -0.5 * LOG2E
    q_refs = (q0_ref, q1_ref, q2_ref, q3_ref)
    heads = [(pair, parity) for pair in range(ATT_PAIRS) for parity in range(2)]

    def scores(pair, parity):
        if parity == 0:
            q = q_refs[pair][...]
            qn = q * lax.rsqrt(_group_mean_sq(q, gmat) + EPS) * (qg_ref[...] * sm_scale)
            scores.qr = _rope(qn, cosq, sinq, lane)
        keep = first if parity == 0 else jnp.logical_not(first)
        qm = jnp.where(keep, scores.qr, 0.0).astype(BF16)
        return _dot_nt(qm, kext_scr[pair // PAIRS_PER_KV])

    halves = []

    def values(pair, parity, p):
        ov = _dot(p, vext_scr[pair // PAIRS_PER_KV, parity])
        halves.append(ov / pltpu.roll(ov, ATT_HEAD_DIM, axis=1))
        if parity == 1:
            o = jnp.where(first, halves[0], halves[1])
            y = o * lax.rsqrt(_group_mean_sq(o, gmat) + EPS) * og_ref[...]
            y_ref[pair] = y.astype(y_ref.dtype)
            halves.clear()

    s = scores(*heads[0])
    pending = None
    for idx, head in enumerate(heads):
        s_next = scores(*heads[idx + 1]) if idx + 1 < len(heads) else None
        if pending is not None:
            values(*pending)
        m = jnp.max(s, axis=-1, keepdims=True)
        pending = (*head, jnp.exp2(s - m).astype(BF16))
        s = s_next
    values(*pending)


def _attn_call(layer, z4, cos, sin, q_gain, k_gain, out_gain, tq):
    _, nb, seq, _ = z4.shape
    nq = seq // tq
    lanes2 = np.arange(2 * LANES)
    grp = (lanes2[:, None] // ATT_HEAD_DIM) == (lanes2[None, :] // ATT_HEAD_DIM)
    gmat = jnp.asarray(grp.astype(np.float32) / ATT_HEAD_DIM, BF16)

    def qspec(pair):
        return pl.BlockSpec((None, None, tq, LANES), lambda b, i: (OB_AQ + pair, b, i, 0))

    def full(cb):
        return pl.BlockSpec((None, None, seq, LANES), lambda b, i: (cb, b, 0, 0))

    vec = pl.BlockSpec((None, 1, LANES), lambda b, i: (layer, 0, 0))
    return pl.pallas_call(
        _attn_kernel,
        grid=(nb, nq),
        in_specs=[
            qspec(0), qspec(1), qspec(2), qspec(3), full(OB_AK), full(OB_AV),
            pl.BlockSpec((tq, LANES), lambda b, i: (i, 0)),
            pl.BlockSpec((tq, LANES), lambda b, i: (i, 0)),
            pl.BlockSpec((seq, LANES), lambda b, i: (0, 0)),
            pl.BlockSpec((seq, LANES), lambda b, i: (0, 0)),
            vec, vec, vec,
            pl.BlockSpec((2 * LANES, 2 * LANES), lambda b, i: (0, 0)),
        ],
        out_specs=pl.BlockSpec((ATT_PAIRS, tq, LANES), lambda b, i: (0, b * nq + i, 0)),
        out_shape=jax.ShapeDtypeStruct((ATT_PAIRS, nb * seq, LANES), BF16),
        scratch_shapes=[
            pltpu.VMEM((ATT_KV_HEADS, seq, LANES), BF16),
            pltpu.VMEM((ATT_KV_HEADS, 2, seq, LANES), BF16),
        ],
        compiler_params=pltpu.CompilerParams(
            dimension_semantics=("arbitrary", "arbitrary"), vmem_limit_bytes=VMEM_LIMIT_BYTES),
        name="attn",
    )(z4, z4, z4, z4, z4, z4, cos, sin, cos, sin, q_gain, k_gain, out_gain, gmat)


def _outffn_kernel(ah_ref, aa_ref, x_ref, mod_ref, g_post_mix_ref, g_pre_ff_ref, g_post_ff_ref,
                   wo_ref, w1_ref, w2_ref, o_ref):
    gate1 = mod_ref[2:3, :]
    shift2 = mod_ref[3:4, :]
    scale2 = mod_ref[4:5, :]
    gate2 = mod_ref[5:6, :]
    d_ff = w1_ref.shape[1]
    fc = 1024
    part_rows = min(ROW_PART, x_ref.shape[0])
    parts = [pl.ds(r0, part_rows) for r0 in range(0, x_ref.shape[0], part_rows)]
    ys = []
    for rows in parts:
        a = jnp.concatenate([ah_ref[j, rows, :] for j in range(ah_ref.shape[0])]
                            + [aa_ref[j, rows, :] for j in range(aa_ref.shape[0])], axis=1)
        ys.append(_dot(a, wo_ref[...]))
    for rows, y in zip(parts, ys):
        x1 = x_ref[rows, :] + gate1 * _row_rms(y, g_post_mix_ref[...])
        h = (_row_rms(x1, g_pre_ff_ref[...]) * (1.0 + scale2) + shift2).astype(BF16)
        acc = None
        for c0 in range(0, d_ff, fc):
            u = jnp.maximum(_dot(h, w1_ref[:, c0:c0 + fc]), 0.0)
            part = _dot((u * u).astype(BF16), w2_ref[c0:c0 + fc, :])
            acc = part if acc is None else acc + part
        o_ref[rows, :] = x1 + gate2 * _row_rms(acc, g_post_ff_ref[...])


def _outffn_call(layer, ah, aa, x2d, mod, g_post_mix, g_pre_ff, g_post_ff, wo_bf, w1_bf, w2_bf,
                 seq, tm):
    rows, d = x2d.shape
    d_ff = w1_bf.shape[2]
    tiles_per_seq = seq // tm
    vec = pl.BlockSpec((None, 1, d), lambda i: (layer, 0, 0))

    def resident(shape):
        return pl.BlockSpec((None,) + shape, lambda i: (layer, 0, 0),
                            pipeline_mode=pl.Buffered(1))

    return pl.pallas_call(
        _outffn_kernel,
        grid=(rows // tm,),
        in_specs=[
            pl.BlockSpec((ah.shape[0], tm, LANES), lambda i: (0, i, 0)),
            pl.BlockSpec((aa.shape[0], tm, LANES), lambda i: (0, i, 0)),
            pl.BlockSpec((tm, d), lambda i: (i, 0)),
            pl.BlockSpec((None, None, N_MOD, d), lambda i: (layer, i // tiles_per_seq, 0, 0)),
            vec, vec, vec,
            resident((d, d)), resident((d, d_ff)), resident((d_ff, d)),
        ],
        out_specs=pl.BlockSpec((tm, d), lambda i: (i, 0)),
        out_shape=jax.ShapeDtypeStruct((rows, d), F32),
        compiler_params=pltpu.CompilerParams(
            dimension_semantics=("arbitrary",), vmem_limit_bytes=VMEM_LIMIT_BYTES),
        name="outffn",
    )(ah, aa, x2d, mod, g_post_mix, g_pre_ff, g_post_ff, wo_bf, w1_bf, w2_bf)


def kernel(x, c, w_ada, ada_layer_bias, hg_lb_logits, pre_mix, post_mix, w_in, w_out, hg_out_gain,
           q_gain, k_gain, att_out_gain, pre_ff, post_ff, w_ff1, w_ff2):
    nb, seq, d = x.shape
    depth = w_in.shape[0]
    assert w_in.shape[2] == N_COL_BLOCKS * LANES and d == HG_WIDTH + ATT_WIDTH
    tm = min(1024, seq)
    tm_in = min(512, seq)
    tq = min(256, seq)

    mod, lb_table = _cond_call(c, w_ada, ada_layer_bias, hg_lb_logits)
    mod = mod.reshape(depth, nb, N_MOD, d)
    lb_table = lb_table.reshape(depth, HG_HEADS, 1, LANES)
    cos, sin = _rope_tables(seq)

    def rows3(a):
        return a.reshape(depth, 1, a.shape[-1])

    def head_pair(g):
        return rows3(jnp.tile(g, (1, LANES // ATT_HEAD_DIM)))

    w_in_bf, w_out_bf = w_in.astype(BF16), w_out.astype(BF16)
    w_ff1_bf, w_ff2_bf = w_ff1.astype(BF16), w_ff2.astype(BF16)
    pre_mix, post_mix, pre_ff, post_ff = map(rows3, (pre_mix, post_mix, pre_ff, post_ff))
    hg_out_gain = rows3(hg_out_gain)
    q_gain, k_gain, att_out_gain = map(head_pair, (q_gain, k_gain, att_out_gain))

    x2d = x.reshape(nb * seq, d)
    for l in range(depth):
        z = _inproj_call(l, x2d, mod, pre_mix, w_in_bf, lb_table, seq, tm_in)
        z4 = z.reshape(N_OUT_BLOCKS, nb, seq, LANES)
        y_hgrn = _hgrn_call(l, z4, hg_out_gain)
        y_att = _attn_call(l, z4, cos, sin, q_gain, k_gain, att_out_gain, tq)
        x2d = _outffn_call(l, y_hgrn, y_att, x2d, mod, post_mix, pre_ff, post_ff,
                           w_out_bf, w_ff1_bf, w_ff2_bf, seq, tm)
    return x2d.reshape(nb, seq, d)
```

```python
import functools

import numpy as np
import jax
import jax.numpy as jnp
from jax import lax
from jax.experimental import pallas as pl
from jax.experimental.pallas import tpu as pltpu

F32 = jnp.float32
BF16 = jnp.bfloat16

LANES = 128
VMEM_LIMIT_BYTES = 56 * 1024 * 1024
ROW_PART = 256

EPS = 1e-6
N_MOD = 6
HG_HEADS = 4
HG_HEAD_DIM = 128
HG_WIDTH = HG_HEADS * HG_HEAD_DIM
HG_CHUNK = 64
HG_GROUP = 8
HG_PREP_ROWS = 256
LOG2E = 1.4426950408889634
HG_LEVEL_HALVES = (32, 16, 8, 4, 2, 1)
ATT_HEADS = 8
ATT_HEAD_DIM = 64
ATT_KV_HEADS = 2
ATT_WIDTH = ATT_HEADS * ATT_HEAD_DIM
KV_WIDTH = ATT_KV_HEADS * ATT_HEAD_DIM
GRID_W = 64
ROPE_THETA = 10000.0

CB_HQ, CB_FFW, CB_FBW, CB_HI, CB_HG = 0, 4, 8, 12, 16
CB_AQ, CB_AK, CB_AV = 20, 24, 25
N_COL_BLOCKS = 26
OB_QS, OB_K, OB_B, OB_V, OB_SG = 0, 4, 12, 20, 24
OB_AQ, OB_AK, OB_AV = 28, 32, 33
N_OUT_BLOCKS = 34
ATT_PAIRS = ATT_WIDTH // LANES
PAIRS_PER_KV = ATT_PAIRS // ATT_KV_HEADS


def _dot(a, b):
    return jnp.dot(a, b, preferred_element_type=F32)


def _dot_nt(a, b):
    return lax.dot_general(a, b, (((1,), (1,)), ((), ())), preferred_element_type=F32)


def _split2(x):
    hi = x.astype(BF16)
    lo = (x - hi.astype(F32)).astype(BF16)
    return hi, lo


def _split3(x):
    hi = x.astype(BF16)
    r1 = x - hi.astype(F32)
    mid = r1.astype(BF16)
    lo = (r1 - mid.astype(F32)).astype(BF16)
    return hi, mid, lo


def _sigmoid(x):
    t = jnp.exp(-jnp.abs(x))
    return jnp.where(x >= 0, 1.0, t) / (1.0 + t)


def _row_rms(x, gain):
    ms = jnp.mean(x * x, axis=-1, keepdims=True)
    return x * lax.rsqrt(ms + EPS) * gain


def _group_mean_sq(x, gmat):
    w = x.shape[1]
    r = _dot(jnp.concatenate(_split2(x * x), axis=1), gmat)
    return r[:, :w] + r[:, w:]


def _cond_kernel(c_ref, w_ref, bias_ref, lbl_ref, mod_ref, lb_ref):
    depth = bias_ref.shape[0]
    c = c_ref[...]
    s = c * _sigmoid(c)
    s_hi = s.astype(BF16).astype(F32)
    s_lo = s - s_hi
    w_hi, w_lo = _split2(w_ref[...])
    nb = c.shape[0]
    r1 = _dot(jnp.concatenate([s_hi, s_lo], axis=0).astype(BF16), w_hi)
    r2 = _dot(jnp.concatenate([s_hi, s_hi], axis=0).astype(BF16), w_lo)
    cond = r1[:nb] + r1[nb:] + r2[:nb]
    for l in range(depth):
        mod_ref[l] = cond + bias_ref[l:l + 1, :]

    rows = [lbl_ref[l:l + 1, :] for l in range(depth)]
    m = functools.reduce(jnp.maximum, rows)
    es = [jnp.exp(r - m) for r in rows]
    tot = functools.reduce(lambda a, b: a + b, es)
    cum = None
    first = None
    for l in range(depth):
        p = es[l] / tot
        cum = p if cum is None else cum + p
        if first is None:
            first = cum
        lb_ref[l:l + 1, :] = cum - first


def _cond_call(c, w_ada, ada_layer_bias, hg_lb_logits):
    nb, d = c.shape
    n_out = w_ada.shape[1]
    depth = ada_layer_bias.shape[0]
    tn = n_out // 4
    return pl.pallas_call(
        _cond_kernel,
        grid=(n_out // tn,),
        in_specs=[
            pl.BlockSpec((nb, d), lambda j: (0, 0)),
            pl.BlockSpec((d, tn), lambda j: (0, j)),
            pl.BlockSpec((depth, tn), lambda j: (0, j)),
            pl.BlockSpec(hg_lb_logits.shape, lambda j: (0, 0)),
        ],
        out_specs=[
            pl.BlockSpec((depth, nb, tn), lambda j: (0, 0, j)),
            pl.BlockSpec(hg_lb_logits.shape, lambda j: (0, 0)),
        ],
        out_shape=[
            jax.ShapeDtypeStruct((depth, nb, n_out), F32),
            jax.ShapeDtypeStruct(hg_lb_logits.shape, F32),
        ],
        compiler_params=pltpu.CompilerParams(
            dimension_semantics=("arbitrary",), vmem_limit_bytes=VMEM_LIMIT_BYTES),
        name="cond",
    )(c, w_ada, ada_layer_bias, hg_lb_logits)


def _silu(z):
    return z / (1.0 + jnp.exp2(z * -LOG2E))


def _hgrn_gate(zf, lb, one_m_lb):
    t = jnp.exp(-jnp.abs(zf))
    neg = zf < 0
    inv = 1.0 / (1.0 + t)
    num = jnp.where(neg, lb + t, lb * t + 1.0)
    g = jnp.where(num > 0, jnp.log(num * inv), zf) * LOG2E
    return one_m_lb * (jnp.where(neg, 1.0, t) * inv), g


def _inproj_kernel(x_ref, mod_ref, gain_ref, w_ref, lb_ref, tri_ref, z_ref):
    shift = mod_ref[0:1, :]
    scale = mod_ref[1:2, :]
    c = HG_CHUNK

    def emit(cb, rows, r0, z, decays):
        if cb >= CB_AQ:
            z_ref[OB_AQ + cb - CB_AQ, rows, :] = z
        elif cb >= CB_HG:
            z_ref[OB_SG + cb - CB_HG, rows, :] = _silu(z)
        elif cb >= CB_HI:
            z_ref[OB_V + cb - CB_HI, rows, :] = z
        elif cb >= CB_FFW:
            d, head = divmod(cb - CB_FFW, HG_HEADS)
            lb = lb_ref[head]
            k, g = _hgrn_gate(z, lb, 1.0 - lb)
            z_ref[OB_K + d * HG_HEADS + head, rows, :] = k
            decays.append((d, head, r0, _split3(g)))
        else:
            z_ref[OB_QS + cb - CB_HQ, rows, :] = _silu(z)

    def cumulate(decays):
        for d, head, r0, parts in decays:
            for jc in range(parts[0].shape[0] // c):
                g3 = jnp.concatenate([p[jc * c:(jc + 1) * c] for p in parts], axis=0)
                z_ref[OB_B + d * HG_HEADS + head, pl.ds(r0 + jc * c, c), :] = _dot(tri_ref[d], g3)

    part_rows = min(ROW_PART, x_ref.shape[0])
    pending = []
    for r0 in range(0, x_ref.shape[0], part_rows):
        rows = pl.ds(r0, part_rows)
        h = _row_rms(x_ref[rows, :], gain_ref[...]) * (1.0 + scale) + shift
        hb = h.astype(BF16)
        decays = []
        for cb in range(0, N_COL_BLOCKS, 2):
            z = _dot(hb, w_ref[:, cb * LANES:(cb + 2) * LANES])
            emit(cb, rows, r0, z[:, :LANES], decays)
            emit(cb + 1, rows, r0, z[:, LANES:], decays)
        cumulate(pending)
        pending = decays
    cumulate(pending)


def _inproj_call(layer, x2d, mod, gain, w_in_bf, lb, seq, tm):
    rows, d = x2d.shape
    n_cols = w_in_bf.shape[2]
    n_blocks = N_OUT_BLOCKS
    tiles_per_seq = seq // tm
    tri = _hgrn_tri()
    return pl.pallas_call(
        _inproj_kernel,
        grid=(rows // tm,),
        in_specs=[
            pl.BlockSpec((tm, d), lambda i: (i, 0)),
            pl.BlockSpec((None, None, N_MOD, d), lambda i: (layer, i // tiles_per_seq, 0, 0)),
            pl.BlockSpec((None, 1, d), lambda i: (layer, 0, 0)),
            pl.BlockSpec((None, d, n_cols), lambda i: (layer, 0, 0),
                         pipeline_mode=pl.Buffered(1)),
            pl.BlockSpec((None, HG_HEADS, 1, LANES), lambda i: (layer, 0, 0, 0)),
            pl.BlockSpec(tri.shape, lambda i: (0, 0, 0)),
        ],
        out_specs=pl.BlockSpec((n_blocks, tm, LANES), lambda i: (0, i, 0)),
        out_shape=jax.ShapeDtypeStruct((n_blocks, rows, LANES), F32),
        compiler_params=pltpu.CompilerParams(
            dimension_semantics=("arbitrary",), vmem_limit_bytes=VMEM_LIMIT_BYTES),
        name="inproj",
    )(x2d, mod, gain, w_in_bf, lb, jnp.asarray(tri, BF16))


def _hgrn_tri():
    t = np.arange(HG_CHUNK)[:, None]
    s = np.arange(HG_CHUNK)[None, :]
    return np.tile(np.stack([s <= t, s >= t]).astype(np.float32), (1, 1, 3))


def _hgrn_masks():
    c = HG_CHUNK
    t = np.arange(c)[:, None]
    s = np.arange(c)[None, :]
    fwd = []
    for half in HG_LEVEL_HALVES:
        same = (t // (2 * half)) == (s // (2 * half))
        t_low = (t // half) % 2 == 1
        s_up = (s // half) % 2 == 0
        fwd.append(same & t_low & s_up)
    fwd.append(t == s)
    fwd = np.stack(fwd).astype(np.float32)
    masks = np.stack([fwd, fwd.transpose(0, 2, 1)])
    return np.tile(masks, (1, 1, 1, 2))


def _hgrn_kernel(qs_ref, kfw_ref, bfw_ref, kbw_ref, bbw_ref, v_ref, sg_ref, gain_ref, mask_ref,
                 y_ref, st_scr, o_scr, v_scr, vt_scr):
    seq = qs_ref.shape[0]
    n_chunks = seq // HG_CHUNK
    c = HG_CHUNK
    grp = HG_GROUP
    slab = min(HG_PREP_ROWS, seq)
    n_sub = max(n for n in (1, 2, 4, 8) if seq % (n * slab) == 0)
    qs_scr = qs_ref
    k_scr = (kfw_ref, kbw_ref)
    b_scr = (bfw_ref, bbw_ref)
    row = lax.broadcasted_iota(jnp.int32, (c, LANES), 0)
    left = lax.broadcasted_iota(jnp.int32, (c, LANES), 1) < c
    sub = lax.broadcasted_iota(jnp.int32, (8, LANES), 0)

    st_scr[...] = jnp.zeros(st_scr.shape, F32)

    def prep(i, carry):
        for j in range(n_sub):
            rows = pl.ds(pl.multiple_of((i * n_sub + j) * slab, slab), slab)
            v = v_ref[rows, :]
            v_scr[rows, :] = v.astype(BF16)
            for jc in range(slab // c):
                vt_scr[(i * n_sub + j) * (slab // c) + jc] = v[jc * c:(jc + 1) * c].T.astype(BF16)
        return carry

    lax.fori_loop(0, seq // (slab * n_sub), prep, 0)

    def mid_reference(d, r0, b, half):
        def brow(r, n):
            return jnp.broadcast_to(b_scr[d][pl.ds(r0 + r, 1), :], (n, LANES))

        if half == c // 2:
            return brow(half, c)
        if half >= 4:
            return jnp.concatenate(
                [brow(base + half, 2 * half) for base in range(0, c, 2 * half)], axis=0)
        assert half == 2
        return jnp.concatenate(
            [jnp.where(sub < 4, brow(base + 2, 8), brow(base + 6, 8))
             for base in range(0, c, 8)], axis=0)

    def trip(i):
        chunk_ids = [i * grp + j for j in range(grp)]
        chunk_ids += [n_chunks - 1 - (i * grp + j) for j in range(grp)]
        insts = [(slot // grp, cid * c) for slot, cid in enumerate(chunk_ids)]
        pairs = [(d, (s, s + 1)) for d in range(2) for s in range(d * grp, (d + 1) * grp, 2)]

        qs_l = [qs_scr[pl.ds(r0, c), :] for _, r0 in insts]
        k_l = [k_scr[d][pl.ds(r0, c), :] for d, r0 in insts]
        v_l = [v_scr[pl.ds(r0, c), :] for _, r0 in insts]
        b_l = [b_scr[d][pl.ds(r0, c), :] for d, r0 in insts]

        o_l = [None] * len(insts)
        for d in range(2):
            st = st_scr[d]
            for slot in range(d * grp, (d + 1) * grp):
                edge = c - 1 if d == 0 else 0
                b_edge = b_scr[d][pl.ds(insts[slot][1] + edge, 1), :]
                qe = (qs_l[slot] * jnp.exp2(b_l[slot])).astype(BF16)
                kd = (k_l[slot] * jnp.exp2(b_edge - b_l[slot])).astype(BF16)
                o_l[slot] = _dot(qe, st.T.astype(BF16))
                st = st * jnp.exp2(b_edge) + _dot(vt_scr[chunk_ids[slot]], kd)
            st_scr[d] = st

        zero = jnp.zeros((c, LANES), BF16)

        def block_diag(xa, xb):
            return jnp.concatenate([jnp.concatenate([xa, zero], axis=1),
                                    jnp.concatenate([zero, xb], axis=1)], axis=0)

        def pair_scores(xa_q, xa_k, xb_q, xb_k):
            return _dot_nt(jnp.concatenate([xa_q, xb_q], axis=1), block_diag(xa_k, xb_k))

        n_lvl = len(HG_LEVEL_HALVES)
        diag_l, adj_l = [], []
        for slot, (d, _) in enumerate(insts):
            qs, k, b = qs_l[slot], k_l[slot], b_l[slot]
            shift = 1 if d == 0 else c - 1
            k_adj = pltpu.roll(k, shift, axis=0)
            decay = jnp.exp2(jnp.minimum(b - pltpu.roll(b, shift, axis=0), 0.0))
            diag_l.append(jnp.sum(qs * k, axis=-1, keepdims=True))
            adj_l.append(jnp.sum(qs * k_adj * decay, axis=-1, keepdims=True))
        sc_l = []
        for d, (sa, sb) in pairs:
            sc_l.append(jnp.where(left, diag_l[sa], diag_l[sb]) * mask_ref[d, n_lvl]
                        + jnp.where(left, adj_l[sa], adj_l[sb]) * mask_ref[d, n_lvl - 1])
        for lvl, half in enumerate(HG_LEVEL_HALVES[:-1]):
            for p, (d, slots) in enumerate(pairs):
                xs = []
                for slot in slots:
                    b = b_l[slot]
                    q_rows = ((row & half) != 0) if d == 0 else ((row & half) == 0)
                    sign = jnp.where(q_rows, 1.0, -1.0)
                    e = jnp.exp2((b - mid_reference(d, insts[slot][1], b, half)) * sign)
                    xs.append((jnp.where(q_rows, qs_l[slot], k_l[slot]) * e).astype(BF16))
                sc_l[p] = sc_l[p] + pair_scores(xs[0], xs[0], xs[1], xs[1]) * mask_ref[d, lvl]

        for p, (d, (sa, sb)) in enumerate(pairs):
            o_intra = _dot(sc_l[p].astype(BF16),
                           block_diag(v_l[sa], v_l[sb]))
            o_scr[d, pl.ds(insts[sa][1], c), :] = o_l[sa] + o_intra[:, :LANES]
            o_scr[d, pl.ds(insts[sb][1], c), :] = o_l[sb] + o_intra[:, LANES:]

    for i in range(n_chunks // grp):
        trip(i)

    gain = gain_ref[...]

    def fin(i, carry):
        for j in range(n_sub):
            rows = pl.ds(pl.multiple_of((i * n_sub + j) * slab, slab), slab)
            o = o_scr[0, rows, :] + o_scr[1, rows, :]
            y_ref[rows, :] = (_row_rms(o, gain) * sg_ref[rows, :]).astype(y_ref.dtype)
        return carry

    lax.fori_loop(0, seq // (slab * n_sub), fin, 0)


def _hgrn_call(layer, z4, out_gain):
    _, nb, seq, _ = z4.shape
    assert seq % (HG_GROUP * HG_CHUNK) == 0 and seq % HG_PREP_ROWS == 0
    masks = _hgrn_masks()

    def zspec(ob0):
        return pl.BlockSpec((None, None, seq, LANES), lambda b, h: (ob0 + h, b, 0, 0))

    y = pl.pallas_call(
        _hgrn_kernel,
        grid=(nb, HG_HEADS),
        in_specs=[
            zspec(OB_QS), zspec(OB_K), zspec(OB_B), zspec(OB_K + HG_HEADS), zspec(OB_B + HG_HEADS),
            zspec(OB_V), zspec(OB_SG),
            pl.BlockSpec((None, 1, LANES), lambda b, h: (layer, 0, 0)),
            pl.BlockSpec(masks.shape, lambda b, h: (0, 0, 0, 0)),
        ],
        out_specs=pl.BlockSpec((None, None, seq, LANES), lambda b, h: (h, b, 0, 0)),
        out_shape=jax.ShapeDtypeStruct((HG_HEADS, nb, seq, LANES), BF16),
        scratch_shapes=[
            pltpu.VMEM((2, HG_HEAD_DIM, HG_HEAD_DIM), F32),
            pltpu.VMEM((2, seq, LANES), F32),
            pltpu.VMEM((seq, LANES), BF16),
            pltpu.VMEM((seq // HG_CHUNK, HG_HEAD_DIM, HG_CHUNK), BF16),
        ],
        compiler_params=pltpu.CompilerParams(
            dimension_semantics=("arbitrary", "arbitrary"), vmem_limit_bytes=VMEM_LIMIT_BYTES),
        name="hgrn",
    )(z4, z4, z4, z4, z4, z4, z4, out_gain, jnp.asarray(masks, F32))
    return y.reshape(HG_HEADS, nb * seq, LANES)


def _rope_tables(seq):
    pos = jnp.arange(seq)
    row = (pos // GRID_W).astype(F32)
    col = (pos % GRID_W).astype(F32)
    half = ATT_HEAD_DIM // 2
    inv_freq = 1.0 / (ROPE_THETA ** (jnp.arange(0, half, 2, dtype=F32) / half))
    ang_r = row[:, None] * inv_freq
    ang_c = col[:, None] * inv_freq
    cos = jnp.concatenate([jnp.cos(ang_r)] * 2 + [jnp.cos(ang_c)] * 2, axis=-1)
    sin = jnp.concatenate([-jnp.sin(ang_r), jnp.sin(ang_r), -jnp.sin(ang_c), jnp.sin(ang_c)], axis=-1)
    reps = LANES // ATT_HEAD_DIM
    return jnp.tile(cos, (1, reps)), jnp.tile(sin, (1, reps))


def _rope(x, cos, sin, lane):
    quarter = ATT_HEAD_DIM // 4
    fwd = pltpu.roll(x, LANES - quarter, axis=1)
    bwd = pltpu.roll(x, quarter, axis=1)
    partner = jnp.where((lane & quarter) == 0, fwd, bwd)
    return x * cos + partner * sin


def _attn_kernel(q0_ref, q1_ref, q2_ref, q3_ref, k_ref, v_ref, cosq_ref, sinq_ref, cosk_ref, sink_ref,
                 qg_ref, kg_ref, og_ref, gmat_ref, y_ref, kext_scr, vext_scr):
    seq = k_ref.shape[0]
    tq = q0_ref.shape[0]
    gmat = gmat_ref[...]

    @pl.when(pl.program_id(1) == 0)
    def _prepare_kv():
        slab = min(512, seq)
        lane = lax.broadcasted_iota(jnp.int32, (slab, LANES), 1)
        first = lane < ATT_HEAD_DIM
        for r0 in range(0, seq, slab):
            rs = slice(r0, r0 + slab)
            k = k_ref[rs, :]
            kn = k * lax.rsqrt(_group_mean_sq(k, gmat) + EPS) * kg_ref[...]
            kr = _rope(kn, cosk_ref[rs, :], sink_ref[rs, :], lane)
            kr_sw = pltpu.roll(kr, ATT_HEAD_DIM, axis=1)
            kext_scr[0, rs, :] = jnp.where(first, kr, kr_sw).astype(BF16)
            kext_scr[1, rs, :] = jnp.where(first, kr_sw, kr).astype(BF16)
            v = v_ref[rs, :]
            v_sw = pltpu.roll(v, ATT_HEAD_DIM, axis=1)
            vext_scr[0, 0, rs, :] = jnp.where(first, v, 1.0).astype(BF16)
            vext_scr[0, 1, rs, :] = jnp.where(first, 1.0, v_sw).astype(BF16)
            vext_scr[1, 0, rs, :] = jnp.where(first, v_sw, 1.0).astype(BF16)
            vext_scr[1, 1, rs, :] = jnp.where(first, 1.0, v).astype(BF16)

    lane = lax.broadcasted_iota(jnp.int32, (tq, LANES), 1)
    first = lane < ATT_HEAD_DIM
    cosq = cosq_ref[...]
    sinq = sinq_ref[...]
    sm_scale = ATT_HEAD_DIM ** -0.5 * LOG2E
    q_refs = (q0_ref, q1_ref, q2_ref, q3_ref)
    heads = [(pair, parity) for pair in range(ATT_PAIRS) for parity in range(2)]

    def scores(pair, parity):
        if parity == 0:
            q = q_refs[pair][...]
            qn = q * lax.rsqrt(_group_mean_sq(q, gmat) + EPS) * (qg_ref[...] * sm_scale)
            scores.qr = _rope(qn, cosq, sinq, lane)
        keep = first if parity == 0 else jnp.logical_not(first)
        qm = jnp.where(keep, scores.qr, 0.0).astype(BF16)
        return _dot_nt(qm, kext_scr[pair // PAIRS_PER_KV])

    halves = []

    def values(pair, parity, p):
        ov = _dot(p, vext_scr[pair // PAIRS_PER_KV, parity])
        halves.append(ov / pltpu.roll(ov, ATT_HEAD_DIM, axis=1))
        if parity == 1:
            o = jnp.where(first, halves[0], halves[1])
            y = o * lax.rsqrt(_group_mean_sq(o, gmat) + EPS) * og_ref[...]
            y_ref[pair] = y.astype(y_ref.dtype)
            halves.clear()

    s = scores(*heads[0])
    pending = None
    for idx, head in enumerate(heads):
        s_next = scores(*heads[idx + 1]) if idx + 1 < len(heads) else None
        if pending is not None:
            values(*pending)
        m = jnp.max(s, axis=-1, keepdims=True)
        pending = (*head, jnp.exp2(s - m).astype(BF16))
        s = s_next
    values(*pending)


def _attn_call(layer, z4, cos, sin, q_gain, k_gain, out_gain, tq):
    _, nb, seq, _ = z4.shape
    nq = seq // tq
    lanes2 = np.arange(2 * LANES)
    grp = (lanes2[:, None] // ATT_HEAD_DIM) == (lanes2[None, :] // ATT_HEAD_DIM)
    gmat = jnp.asarray(grp.astype(np.float32) / ATT_HEAD_DIM, BF16)

    def qspec(pair):
        return pl.BlockSpec((None, None, tq, LANES), lambda b, i: (OB_AQ + pair, b, i, 0))

    def full(cb):
        return pl.BlockSpec((None, None, seq, LANES), lambda b, i: (cb, b, 0, 0))

    vec = pl.BlockSpec((None, 1, LANES), lambda b, i: (layer, 0, 0))
    return pl.pallas_call(
        _attn_kernel,
        grid=(nb, nq),
        in_specs=[
            qspec(0), qspec(1), qspec(2), qspec(3), full(OB_AK), full(OB_AV),
            pl.BlockSpec((tq, LANES), lambda b, i: (i, 0)),
            pl.BlockSpec((tq, LANES), lambda b, i: (i, 0)),
            pl.BlockSpec((seq, LANES), lambda b, i: (0, 0)),
            pl.BlockSpec((seq, LANES), lambda b, i: (0, 0)),
            vec, vec, vec,
            pl.BlockSpec((2 * LANES, 2 * LANES), lambda b, i: (0, 0)),
        ],
        out_specs=pl.BlockSpec((ATT_PAIRS, tq, LANES), lambda b, i: (0, b * nq + i, 0)),
        out_shape=jax.ShapeDtypeStruct((ATT_PAIRS, nb * seq, LANES), BF16),
        scratch_shapes=[
            pltpu.VMEM((ATT_KV_HEADS, seq, LANES), BF16),
            pltpu.VMEM((ATT_KV_HEADS, 2, seq, LANES), BF16),
        ],
        compiler_params=pltpu.CompilerParams(
            dimension_semantics=("arbitrary", "arbitrary"), vmem_limit_bytes=VMEM_LIMIT_BYTES),
        name="attn",
    )(z4, z4, z4, z4, z4, z4, cos, sin, cos, sin, q_gain, k_gain, out_gain, gmat)


def _outffn_kernel(ah_ref, aa_ref, x_ref, mod_ref, g_post_mix_ref, g_pre_ff_ref, g_post_ff_ref,
                   wo_ref, w1_ref, w2_ref, o_ref):
    gate1 = mod_ref[2:3, :]
    shift2 = mod_ref[3:4, :]
    scale2 = mod_ref[4:5, :]
    gate2 = mod_ref[5:6, :]
    d_ff = w1_ref.shape[1]
    fc = 1024
    part_rows = min(ROW_PART, x_ref.shape[0])
    parts = [pl.ds(r0, part_rows) for r0 in range(0, x_ref.shape[0], part_rows)]
    ys = []
    for rows in parts:
        a = jnp.concatenate([ah_ref[j, rows, :] for j in range(ah_ref.shape[0])]
                            + [aa_ref[j, rows, :] for j in range(aa_ref.shape[0])], axis=1)
        ys.append(_dot(a, wo_ref[...]))
    for rows, y in zip(parts, ys):
        x1 = x_ref[rows, :] + gate1 * _row_rms(y, g_post_mix_ref[...])
        h = (_row_rms(x1, g_pre_ff_ref[...]) * (1.0 + scale2) + shift2).astype(BF16)
        acc = None
        for c0 in range(0, d_ff, fc):
            u = jnp.maximum(_dot(h, w1_ref[:, c0:c0 + fc]), 0.0)
            part = _dot((u * u).astype(BF16), w2_ref[c0:c0 + fc, :])
            acc = part if acc is None else acc + part
        o_ref[rows, :] = x1 + gate2 * _row_rms(acc, g_post_ff_ref[...])


def _outffn_call(layer, ah, aa, x2d, mod, g_post_mix, g_pre_ff, g_post_ff, wo_bf, w1_bf, w2_bf,
                 seq, tm):
    rows, d = x2d.shape
    d_ff = w1_bf.shape[2]
    tiles_per_seq = seq // tm
    vec = pl.BlockSpec((None, 1, d), lambda i: (layer, 0, 0))

    def resident(shape):
        return pl.BlockSpec((None,) + shape, lambda i: (layer, 0, 0),
                            pipeline_mode=pl.Buffered(1))

    return pl.pallas_call(
        _outffn_kernel,
        grid=(rows // tm,),
        in_specs=[
            pl.BlockSpec((ah.shape[0], tm, LANES), lambda i: (0, i, 0)),
            pl.BlockSpec((aa.shape[0], tm, LANES), lambda i: (0, i, 0)),
            pl.BlockSpec((tm, d), lambda i: (i, 0)),
            pl.BlockSpec((None, None, N_MOD, d), lambda i: (layer, i // tiles_per_seq, 0, 0)),
            vec, vec, vec,
            resident((d, d)), resident((d, d_ff)), resident((d_ff, d)),
        ],
        out_specs=pl.BlockSpec((tm, d), lambda i: (i, 0)),
        out_shape=jax.ShapeDtypeStruct((rows, d), F32),
        compiler_params=pltpu.CompilerParams(
            dimension_semantics=("arbitrary",), vmem_limit_bytes=VMEM_LIMIT_BYTES),
        name="outffn",
    )(ah, aa, x2d, mod, g_post_mix, g_pre_ff, g_post_ff, wo_bf, w1_bf, w2_bf)


def kernel(x, c, w_ada, ada_layer_bias, hg_lb_logits, pre_mix, post_mix, w_in, w_out, hg_out_gain,
           q_gain, k_gain, att_out_gain, pre_ff, post_ff, w_ff1, w_ff2):
    nb, seq, d = x.shape
    depth = w_in.shape[0]
    assert w_in.shape[2] == N_COL_BLOCKS * LANES and d == HG_WIDTH + ATT_WIDTH
    tm = min(1024, seq)
    tm_in = min(512, seq)
    tq = min(256, seq)

    mod, lb_table = _cond_call(c, w_ada, ada_layer_bias, hg_lb_logits)
    mod = mod.reshape(depth, nb, N_MOD, d)
    lb_table = lb_table.reshape(depth, HG_HEADS, 1, LANES)
    cos, sin = _rope_tables(seq)

    def rows3(a):
        return a.reshape(depth, 1, a.shape[-1])

    def head_pair(g):
        return rows3(jnp.tile(g, (1, LANES // ATT_HEAD_DIM)))

    w_in_bf, w_out_bf = w_in.astype(BF16), w_out.astype(BF16)
    w_ff1_bf, w_ff2_bf = w_ff1.astype(BF16), w_ff2.astype(BF16)
    pre_mix, post_mix, pre_ff, post_ff = map(rows3, (pre_mix, post_mix, pre_ff, post_ff))
    hg_out_gain = rows3(hg_out_gain)
    q_gain, k_gain, att_out_gain = map(head_pair, (q_gain, k_gain, att_out_gain))

    x2d = x.reshape(nb * seq, d)
    for l in range(depth):
        z = _inproj_call(l, x2d, mod, pre_mix, w_in_bf, lb_table, seq, tm_in)
        z4 = z.reshape(N_OUT_BLOCKS, nb, seq, LANES)
        y_hgrn = _hgrn_call(l, z4, hg_out_gain)
        y_att = _attn_call(l, z4, cos, sin, q_gain, k_gain, att_out_gain, tq)
        x2d = _outffn_call(l, y_hgrn, y_att, x2d, mod, post_mix, pre_ff, post_ff,
                           w_out_bf, w_ff1_bf, w_ff2_bf, seq, tm)
    return x2d.reshape(nb, seq, d)
```

```python
import functools

import numpy as np
import jax
import jax.numpy as jnp
from jax import lax
from jax.experimental import pallas as pl
from jax.experimental.pallas import tpu as pltpu

F32 = jnp.float32
BF16 = jnp.bfloat16

LANES = 128
VMEM_LIMIT_BYTES = 56 * 1024 * 1024
ROW_PART = 256

EPS = 1e-6
N_MOD = 6
HG_HEADS = 4
HG_HEAD_DIM = 128
HG_WIDTH = HG_HEADS * HG_HEAD_DIM
HG_CHUNK = 64
HG_GROUP = 8
HG_PREP_ROWS = 256
LOG2E = 1.4426950408889634
HG_LEVEL_HALVES = (32, 16, 8, 4, 2, 1)
ATT_HEADS = 8
ATT_HEAD_DIM = 64
ATT_KV_HEADS = 2
ATT_WIDTH = ATT_HEADS * ATT_HEAD_DIM
KV_WIDTH = ATT_KV_HEADS * ATT_HEAD_DIM
GRID_W = 64
ROPE_THETA = 10000.0
ATT_SUB_ROWS = 256

CB_HQ, CB_FFW, CB_FBW, CB_HI, CB_HG = 0, 4, 8, 12, 16
CB_AQ, CB_AK, CB_AV = 20, 24, 25
N_COL_BLOCKS = 26
OB_QS, OB_K, OB_B, OB_V, OB_SG = 0, 4, 12, 20, 24
OB_AQ, OB_AK, OB_AV = 28, 32, 33
N_OUT_BLOCKS = 34
ATT_PAIRS = ATT_WIDTH // LANES
PAIRS_PER_KV = ATT_PAIRS // ATT_KV_HEADS


def _dot(a, b):
    return jnp.dot(a, b, preferred_element_type=F32)


def _dot_nt(a, b):
    return lax.dot_general(a, b, (((1,), (1,)), ((), ())), preferred_element_type=F32)


def _split2(x):
    hi = x.astype(BF16)
    lo = (x - hi.astype(F32)).astype(BF16)
    return hi, lo


def _split3(x):
    hi = x.astype(BF16)
    r1 = x - hi.astype(F32)
    mid = r1.astype(BF16)
    lo = (r1 - mid.astype(F32)).astype(BF16)
    return hi, mid, lo


def _sigmoid(x):
    t = jnp.exp(-jnp.abs(x))
    return jnp.where(x >= 0, 1.0, t) / (1.0 + t)


def _row_rms(x, gain):
    ms = jnp.mean(x * x, axis=-1, keepdims=True)
    return x * lax.rsqrt(ms + EPS) * gain


def _group_mean_sq(x, gmat):
    w = x.shape[1]
    r = _dot(jnp.concatenate(_split2(x * x), axis=1), gmat)
    return r[:, :w] + r[:, w:]


def _cond_kernel(c_ref, w_ref, bias_ref, lbl_ref, mod_ref, lb_ref):
    depth = bias_ref.shape[0]
    c = c_ref[...]
    s = c * _sigmoid(c)
    s_hi = s.astype(BF16).astype(F32)
    s_lo = s - s_hi
    w_hi, w_lo = _split2(w_ref[...])
    nb = c.shape[0]
    r1 = _dot(jnp.concatenate([s_hi, s_lo], axis=0).astype(BF16), w_hi)
    r2 = _dot(jnp.concatenate([s_hi, s_hi], axis=0).astype(BF16), w_lo)
    cond = r1[:nb] + r1[nb:] + r2[:nb]
    for l in range(depth):
        mod_ref[l] = cond + bias_ref[l:l + 1, :]

    rows = [lbl_ref[l:l + 1, :] for l in range(depth)]
    m = functools.reduce(jnp.maximum, rows)
    es = [jnp.exp(r - m) for r in rows]
    tot = functools.reduce(lambda a, b: a + b, es)
    cum = None
    first = None
    for l in range(depth):
        p = es[l] / tot
        cum = p if cum is None else cum + p
        if first is None:
            first = cum
        lb_ref[l:l + 1, :] = cum - first


def _cond_call(c, w_ada, ada_layer_bias, hg_lb_logits):
    nb, d = c.shape
    n_out = w_ada.shape[1]
    depth = ada_layer_bias.shape[0]
    tn = n_out // 4
    return pl.pallas_call(
        _cond_kernel,
        grid=(n_out // tn,),
        in_specs=[
            pl.BlockSpec((nb, d), lambda j: (0, 0)),
            pl.BlockSpec((d, tn), lambda j: (0, j)),
            pl.BlockSpec((depth, tn), lambda j: (0, j)),
            pl.BlockSpec(hg_lb_logits.shape, lambda j: (0, 0)),
        ],
        out_specs=[
            pl.BlockSpec((depth, nb, tn), lambda j: (0, 0, j)),
            pl.BlockSpec(hg_lb_logits.shape, lambda j: (0, 0)),
        ],
        out_shape=[
            jax.ShapeDtypeStruct((depth, nb, n_out), F32),
            jax.ShapeDtypeStruct(hg_lb_logits.shape, F32),
        ],
        compiler_params=pltpu.CompilerParams(
            dimension_semantics=("arbitrary",), vmem_limit_bytes=VMEM_LIMIT_BYTES),
        name="cond",
    )(c, w_ada, ada_layer_bias, hg_lb_logits)


def _silu(z):
    return z / (1.0 + jnp.exp2(z * -LOG2E))


def _hgrn_gate(zf, lb, one_m_lb):
    t = jnp.exp(-jnp.abs(zf))
    neg = zf < 0
    inv = 1.0 / (1.0 + t)
    num = jnp.where(neg, lb + t, lb * t + 1.0)
    g = jnp.where(num > 0, jnp.log(num * inv), zf) * LOG2E
    return one_m_lb * (jnp.where(neg, 1.0, t) * inv), g


def _inproj_kernel(x_ref, mod_ref, gain_ref, w_ref, lb_ref, tri_ref, z_ref):
    shift = mod_ref[0:1, :]
    scale = mod_ref[1:2, :]
    c = HG_CHUNK

    def emit(cb, rows, r0, z, decays):
        if cb >= CB_AQ:
            z_ref[OB_AQ + cb - CB_AQ, rows, :] = z
        elif cb >= CB_HG:
            z_ref[OB_SG + cb - CB_HG, rows, :] = _silu(z)
        elif cb >= CB_HI:
            z_ref[OB_V + cb - CB_HI, rows, :] = z
        elif cb >= CB_FFW:
            d, head = divmod(cb - CB_FFW, HG_HEADS)
            lb = lb_ref[head]
            k, g = _hgrn_gate(z, lb, 1.0 - lb)
            z_ref[OB_K + d * HG_HEADS + head, rows, :] = k
            decays.append((d, head, r0, _split3(g)))
        else:
            z_ref[OB_QS + cb - CB_HQ, rows, :] = _silu(z)

    def cumulate(decays):
        for d, head, r0, parts in decays:
            for jc in range(parts[0].shape[0] // c):
                g3 = jnp.concatenate([p[jc * c:(jc + 1) * c] for p in parts], axis=0)
                z_ref[OB_B + d * HG_HEADS + head, pl.ds(r0 + jc * c, c), :] = _dot(tri_ref[d], g3)

    part_rows = min(ROW_PART, x_ref.shape[0])
    pending = []
    for r0 in range(0, x_ref.shape[0], part_rows):
        rows = pl.ds(r0, part_rows)
        h = _row_rms(x_ref[rows, :], gain_ref[...]) * (1.0 + scale) + shift
        hb = h.astype(BF16)
        decays = []
        for cb in range(0, N_COL_BLOCKS, 2):
            z = _dot(hb, w_ref[:, cb * LANES:(cb + 2) * LANES])
            emit(cb, rows, r0, z[:, :LANES], decays)
            emit(cb + 1, rows, r0, z[:, LANES:], decays)
        cumulate(pending)
        pending = decays
    cumulate(pending)


def _inproj_call(layer, x2d, mod, gain, w_in_bf, lb, seq, tm):
    rows, d = x2d.shape
    n_cols = w_in_bf.shape[2]
    n_blocks = N_OUT_BLOCKS
    tiles_per_seq = seq // tm
    tri = _hgrn_tri()
    return pl.pallas_call(
        _inproj_kernel,
        grid=(rows // tm,),
        in_specs=[
            pl.BlockSpec((tm, d), lambda i: (i, 0)),
            pl.BlockSpec((None, None, N_MOD, d), lambda i: (layer, i // tiles_per_seq, 0, 0)),
            pl.BlockSpec((None, 1, d), lambda i: (layer, 0, 0)),
            pl.BlockSpec((None, d, n_cols), lambda i: (layer, 0, 0),
                         pipeline_mode=pl.Buffered(1)),
            pl.BlockSpec((None, HG_HEADS, 1, LANES), lambda i: (layer, 0, 0, 0)),
            pl.BlockSpec(tri.shape, lambda i: (0, 0, 0)),
        ],
        out_specs=pl.BlockSpec((n_blocks, tm, LANES), lambda i: (0, i, 0)),
        out_shape=jax.ShapeDtypeStruct((n_blocks, rows, LANES), F32),
        compiler_params=pltpu.CompilerParams(
            dimension_semantics=("arbitrary",), vmem_limit_bytes=VMEM_LIMIT_BYTES),
        name="inproj",
    )(x2d, mod, gain, w_in_bf, lb, jnp.asarray(tri, BF16))


def _hgrn_tri():
    t = np.arange(HG_CHUNK)[:, None]
    s = np.arange(HG_CHUNK)[None, :]
    return np.tile(np.stack([s <= t, s >= t]).astype(np.float32), (1, 1, 3))


def _hgrn_masks():
    c = HG_CHUNK
    t = np.arange(c)[:, None]
    s = np.arange(c)[None, :]
    fwd = []
    for half in HG_LEVEL_HALVES:
        same = (t // (2 * half)) == (s // (2 * half))
        t_low = (t // half) % 2 == 1
        s_up = (s // half) % 2 == 0
        fwd.append(same & t_low & s_up)
    fwd.append(t == s)
    fwd = np.stack(fwd).astype(np.float32)
    masks = np.stack([fwd, fwd.transpose(0, 2, 1)])
    return np.tile(masks, (1, 1, 1, 2))


def _hgrn_kernel(qs_ref, kfw_ref, bfw_ref, kbw_ref, bbw_ref, v_ref, sg_ref, gain_ref, mask_ref,
                 y_ref, st_scr, o_scr, v_scr, vt_scr):
    seq = qs_ref.shape[0]
    n_chunks = seq // HG_CHUNK
    c = HG_CHUNK
    grp = HG_GROUP
    slab = min(HG_PREP_ROWS, seq)
    n_sub = max(n for n in (1, 2, 4, 8) if seq % (n * slab) == 0)
    qs_scr = qs_ref
    k_scr = (kfw_ref, kbw_ref)
    b_scr = (bfw_ref, bbw_ref)
    row = lax.broadcasted_iota(jnp.int32, (c, LANES), 0)
    left = lax.broadcasted_iota(jnp.int32, (c, LANES), 1) < c
    sub = lax.broadcasted_iota(jnp.int32, (8, LANES), 0)

    st_scr[...] = jnp.zeros(st_scr.shape, F32)

    def prep(i, carry):
        for j in range(n_sub):
            rows = pl.ds(pl.multiple_of((i * n_sub + j) * slab, slab), slab)
            v = v_ref[rows, :]
            v_scr[rows, :] = v.astype(BF16)
            for jc in range(slab // c):
                vt_scr[(i * n_sub + j) * (slab // c) + jc] = v[jc * c:(jc + 1) * c].T.astype(BF16)
        return carry

    lax.fori_loop(0, seq // (slab * n_sub), prep, 0)

    def mid_reference(d, r0, b, half):
        def brow(r, n):
            return jnp.broadcast_to(b_scr[d][pl.ds(r0 + r, 1), :], (n, LANES))

        if half == c // 2:
            return brow(half, c)
        if half >= 4:
            return jnp.concatenate(
                [brow(base + half, 2 * half) for base in range(0, c, 2 * half)], axis=0)
        assert half == 2
        return jnp.concatenate(
            [jnp.where(sub < 4, brow(base + 2, 8), brow(base + 6, 8))
             for base in range(0, c, 8)], axis=0)

    def trip(i):
        chunk_ids = [i * grp + j for j in range(grp)]
        chunk_ids += [n_chunks - 1 - (i * grp + j) for j in range(grp)]
        insts = [(slot // grp, cid * c) for slot, cid in enumerate(chunk_ids)]
        pairs = [(d, (s, s + 1)) for d in range(2) for s in range(d * grp, (d + 1) * grp, 2)]

        qs_l = [qs_scr[pl.ds(r0, c), :] for _, r0 in insts]
        k_l = [k_scr[d][pl.ds(r0, c), :] for d, r0 in insts]
        v_l = [v_scr[pl.ds(r0, c), :] for _, r0 in insts]
        b_l = [b_scr[d][pl.ds(r0, c), :] for d, r0 in insts]

        o_l = [None] * len(insts)
        for d in range(2):
            st = st_scr[d]
            for slot in range(d * grp, (d + 1) * grp):
                edge = c - 1 if d == 0 else 0
                b_edge = b_scr[d][pl.ds(insts[slot][1] + edge, 1), :]
                qe = (qs_l[slot] * jnp.exp2(b_l[slot])).astype(BF16)
                kd = (k_l[slot] * jnp.exp2(b_edge - b_l[slot])).astype(BF16)
                o_l[slot] = _dot(qe, st.T.astype(BF16))
                st = st * jnp.exp2(b_edge) + _dot(vt_scr[chunk_ids[slot]], kd)
            st_scr[d] = st

        zero = jnp.zeros((c, LANES), BF16)

        def block_diag(xa, xb):
            return jnp.concatenate([jnp.concatenate([xa, zero], axis=1),
                                    jnp.concatenate([zero, xb], axis=1)], axis=0)

        def pair_scores(xa_q, xa_k, xb_q, xb_k):
            return _dot_nt(jnp.concatenate([xa_q, xb_q], axis=1), block_diag(xa_k, xb_k))

        n_lvl = len(HG_LEVEL_HALVES)
        diag_l, adj_l = [], []
        for slot, (d, _) in enumerate(insts):
            qs, k, b = qs_l[slot], k_l[slot], b_l[slot]
            shift = 1 if d == 0 else c - 1
            k_adj = pltpu.roll(k, shift, axis=0)
            decay = jnp.exp2(jnp.minimum(b - pltpu.roll(b, shift, axis=0), 0.0))
            diag_l.append(jnp.sum(qs * k, axis=-1, keepdims=True))
            adj_l.append(jnp.sum(qs * k_adj * decay, axis=-1, keepdims=True))
        sc_l = []
        for d, (sa, sb) in pairs:
            sc_l.append(jnp.where(left, diag_l[sa], diag_l[sb]) * mask_ref[d, n_lvl]
                        + jnp.where(left, adj_l[sa], adj_l[sb]) * mask_ref[d, n_lvl - 1])
        for lvl, half in enumerate(HG_LEVEL_HALVES[:-1]):
            for p, (d, slots) in enumerate(pairs):
                xs = []
                for slot in slots:
                    b = b_l[slot]
                    q_rows = ((row & half) != 0) if d == 0 else ((row & half) == 0)
                    sign = jnp.where(q_rows, 1.0, -1.0)
                    e = jnp.exp2((b - mid_reference(d, insts[slot][1], b, half)) * sign)
                    xs.append((jnp.where(q_rows, qs_l[slot], k_l[slot]) * e).astype(BF16))
                sc_l[p] = sc_l[p] + pair_scores(xs[0], xs[0], xs[1], xs[1]) * mask_ref[d, lvl]

        for p, (d, (sa, sb)) in enumerate(pairs):
            o_intra = _dot(sc_l[p].astype(BF16),
                           block_diag(v_l[sa], v_l[sb]))
            o_scr[d, pl.ds(insts[sa][1], c), :] = o_l[sa] + o_intra[:, :LANES]
            o_scr[d, pl.ds(insts[sb][1], c), :] = o_l[sb] + o_intra[:, LANES:]

    for i in range(n_chunks // grp):
        trip(i)

    gain = gain_ref[...]

    def fin(i, carry):
        for j in range(n_sub):
            rows = pl.ds(pl.multiple_of((i * n_sub + j) * slab, slab), slab)
            o = o_scr[0, rows, :] + o_scr[1, rows, :]
            y_ref[rows, :] = (_row_rms(o, gain) * sg_ref[rows, :]).astype(y_ref.dtype)
        return carry

    lax.fori_loop(0, seq // (slab * n_sub), fin, 0)


def _hgrn_call(layer, z4, out_gain):
    _, nb, seq, _ = z4.shape
    assert seq % (HG_GROUP * HG_CHUNK) == 0 and seq % HG_PREP_ROWS == 0
    masks = _hgrn_masks()

    def zspec(ob0):
        return pl.BlockSpec((None, None, seq, LANES), lambda b, h: (ob0 + h, b, 0, 0))

    y = pl.pallas_call(
        _hgrn_kernel,
        grid=(nb, HG_HEADS),
        in_specs=[
            zspec(OB_QS), zspec(OB_K), zspec(OB_B), zspec(OB_K + HG_HEADS), zspec(OB_B + HG_HEADS),
            zspec(OB_V), zspec(OB_SG),
            pl.BlockSpec((None, 1, LANES), lambda b, h: (layer, 0, 0)),
            pl.BlockSpec(masks.shape, lambda b, h: (0, 0, 0, 0)),
        ],
        out_specs=pl.BlockSpec((None, None, seq, LANES), lambda b, h: (h, b, 0, 0)),
        out_shape=jax.ShapeDtypeStruct((HG_HEADS, nb, seq, LANES), BF16),
        scratch_shapes=[
            pltpu.VMEM((2, HG_HEAD_DIM, HG_HEAD_DIM), F32),
            pltpu.VMEM((2, seq, LANES), F32),
            pltpu.VMEM((seq, LANES), BF16),
            pltpu.VMEM((seq // HG_CHUNK, HG_HEAD_DIM, HG_CHUNK), BF16),
        ],
        compiler_params=pltpu.CompilerParams(
            dimension_semantics=("arbitrary", "arbitrary"), vmem_limit_bytes=VMEM_LIMIT_BYTES),
        name="hgrn",
    )(z4, z4, z4, z4, z4, z4, z4, out_gain, jnp.asarray(masks, F32))
    return y.reshape(HG_HEADS, nb * seq, LANES)


def _rope_tables(seq):
    pos = jnp.arange(seq)
    row = (pos // GRID_W).astype(F32)
    col = (pos % GRID_W).astype(F32)
    half = ATT_HEAD_DIM // 2
    inv_freq = 1.0 / (ROPE_THETA ** (jnp.arange(0, half, 2, dtype=F32) / half))
    ang_r = row[:, None] * inv_freq
    ang_c = col[:, None] * inv_freq
    cos = jnp.concatenate([jnp.cos(ang_r)] * 2 + [jnp.cos(ang_c)] * 2, axis=-1)
    sin = jnp.concatenate([-jnp.sin(ang_r), jnp.sin(ang_r), -jnp.sin(ang_c), jnp.sin(ang_c)], axis=-1)
    reps = LANES // ATT_HEAD_DIM
    return jnp.tile(cos, (1, reps)), jnp.tile(sin, (1, reps))


def _rope(x, cos, sin, lane):
    quarter = ATT_HEAD_DIM // 4
    fwd = pltpu.roll(x, LANES - quarter, axis=1)
    bwd = pltpu.roll(x, quarter, axis=1)
    partner = jnp.where((lane & quarter) == 0, fwd, bwd)
    return x * cos + partner * sin


def _attn_kernel(q0_ref, q1_ref, q2_ref, q3_ref, k_ref, v_ref, cosq_ref, sinq_ref, cosk_ref, sink_ref,
                 qg_ref, kg_ref, og_ref, gmat_ref, y_ref, kext_scr, vext_scr):
    seq = k_ref.shape[0]
    tq = q0_ref.shape[0]
    gmat = gmat_ref[...]

    @pl.when(pl.program_id(1) == 0)
    def _prepare_kv():
        slab = min(512, seq)
        lane = lax.broadcasted_iota(jnp.int32, (slab, LANES), 1)
        first = lane < ATT_HEAD_DIM
        for r0 in range(0, seq, slab):
            rs = slice(r0, r0 + slab)
            k = k_ref[rs, :]
            kn = k * lax.rsqrt(_group_mean_sq(k, gmat) + EPS) * kg_ref[...]
            kr = _rope(kn, cosk_ref[rs, :], sink_ref[rs, :], lane)
            kr_sw = pltpu.roll(kr, ATT_HEAD_DIM, axis=1)
            kext_scr[0, rs, :] = jnp.where(first, kr, kr_sw).astype(BF16)
            kext_scr[1, rs, :] = jnp.where(first, kr_sw, kr).astype(BF16)
            v = v_ref[rs, :]
            v_sw = pltpu.roll(v, ATT_HEAD_DIM, axis=1)
            vext_scr[0, 0, rs, :] = jnp.where(first, v, 1.0).astype(BF16)
            vext_scr[0, 1, rs, :] = jnp.where(first, 1.0, v_sw).astype(BF16)
            vext_scr[1, 0, rs, :] = jnp.where(first, v_sw, 1.0).astype(BF16)
            vext_scr[1, 1, rs, :] = jnp.where(first, 1.0, v).astype(BF16)

    sub_rows = min(ATT_SUB_ROWS, tq)
    lane = lax.broadcasted_iota(jnp.int32, (sub_rows, LANES), 1)
    first = lane < ATT_HEAD_DIM
    sm_scale = ATT_HEAD_DIM ** -0.5 * LOG2E
    q_refs = (q0_ref, q1_ref, q2_ref, q3_ref)
    heads = [(r0, pair, parity) for r0 in range(0, tq, sub_rows)
             for pair in range(ATT_PAIRS) for parity in range(2)]

    def scores(r0, pair, parity):
        if parity == 0:
            rows = pl.ds(r0, sub_rows)
            q = q_refs[pair][rows, :]
            qn = q * lax.rsqrt(_group_mean_sq(q, gmat) + EPS) * (qg_ref[...] * sm_scale)
            scores.qr = _rope(qn, cosq_ref[rows, :], sinq_ref[rows, :], lane)
        keep = first if parity == 0 else jnp.logical_not(first)
        qm = jnp.where(keep, scores.qr, 0.0).astype(BF16)
        return _dot_nt(qm, kext_scr[pair // PAIRS_PER_KV])

    halves = []

    def values(r0, pair, parity, p):
        ov = _dot(p, vext_scr[pair // PAIRS_PER_KV, parity])
        halves.append(ov / pltpu.roll(ov, ATT_HEAD_DIM, axis=1))
        if parity == 1:
            o = jnp.where(first, halves[0], halves[1])
            y = o * lax.rsqrt(_group_mean_sq(o, gmat) + EPS) * og_ref[...]
            y_ref[pair, pl.ds(r0, sub_rows), :] = y.astype(y_ref.dtype)
            halves.clear()

    s = scores(*heads[0])
    pending = None
    for idx, head in enumerate(heads):
        s_next = scores(*heads[idx + 1]) if idx + 1 < len(heads) else None
        if pending is not None:
            values(*pending)
        m = jnp.max(s, axis=-1, keepdims=True)
        pending = (*head, jnp.exp2(s - m).astype(BF16))
        s = s_next
    values(*pending)


def _attn_call(layer, z4, cos, sin, q_gain, k_gain, out_gain, tq):
    _, nb, seq, _ = z4.shape
    nq = seq // tq
    lanes2 = np.arange(2 * LANES)
    grp = (lanes2[:, None] // ATT_HEAD_DIM) == (lanes2[None, :] // ATT_HEAD_DIM)
    gmat = jnp.asarray(grp.astype(np.float32) / ATT_HEAD_DIM, BF16)

    def qspec(pair):
        return pl.BlockSpec((None, None, tq, LANES), lambda b, i: (OB_AQ + pair, b, i, 0))

    def full(cb):
        return pl.BlockSpec((None, None, seq, LANES), lambda b, i: (cb, b, 0, 0))

    vec = pl.BlockSpec((None, 1, LANES), lambda b, i: (layer, 0, 0))
    return pl.pallas_call(
        _attn_kernel,
        grid=(nb, nq),
        in_specs=[
            qspec(0), qspec(1), qspec(2), qspec(3), full(OB_AK), full(OB_AV),
            pl.BlockSpec((tq, LANES), lambda b, i: (i, 0)),
            pl.BlockSpec((tq, LANES), lambda b, i: (i, 0)),
            pl.BlockSpec((seq, LANES), lambda b, i: (0, 0)),
            pl.BlockSpec((seq, LANES), lambda b, i: (0, 0)),
            vec, vec, vec,
            pl.BlockSpec((2 * LANES, 2 * LANES), lambda b, i: (0, 0)),
        ],
        out_specs=pl.BlockSpec((ATT_PAIRS, tq, LANES), lambda b, i: (0, b * nq + i, 0)),
        out_shape=jax.ShapeDtypeStruct((ATT_PAIRS, nb * seq, LANES), BF16),
        scratch_shapes=[
            pltpu.VMEM((ATT_KV_HEADS, seq, LANES), BF16),
            pltpu.VMEM((ATT_KV_HEADS, 2, seq, LANES), BF16),
        ],
        compiler_params=pltpu.CompilerParams(
            dimension_semantics=("arbitrary", "arbitrary"), vmem_limit_bytes=VMEM_LIMIT_BYTES),
        name="attn",
    )(z4, z4, z4, z4, z4, z4, cos, sin, cos, sin, q_gain, k_gain, out_gain, gmat)


def _outffn_kernel(ah_ref, aa_ref, x_ref, mod_ref, g_post_mix_ref, g_pre_ff_ref, g_post_ff_ref,
                   wo_ref, w1_ref, w2_ref, o_ref):
    gate1 = mod_ref[2:3, :]
    shift2 = mod_ref[3:4, :]
    scale2 = mod_ref[4:5, :]
    gate2 = mod_ref[5:6, :]
    d_ff = w1_ref.shape[1]
    fc = 1024
    part_rows = min(ROW_PART, x_ref.shape[0])
    parts = [pl.ds(r0, part_rows) for r0 in range(0, x_ref.shape[0], part_rows)]
    ys = []
    for rows in parts:
        a = jnp.concatenate([ah_ref[j, rows, :] for j in range(ah_ref.shape[0])]
                            + [aa_ref[j, rows, :] for j in range(aa_ref.shape[0])], axis=1)
        ys.append(_dot(a, wo_ref[...]))
    for rows, y in zip(parts, ys):
        x1 = x_ref[rows, :] + gate1 * _row_rms(y, g_post_mix_ref[...])
        h = (_row_rms(x1, g_pre_ff_ref[...]) * (1.0 + scale2) + shift2).astype(BF16)
        acc = None
        for c0 in range(0, d_ff, fc):
            u = jnp.maximum(_dot(h, w1_ref[:, c0:c0 + fc]), 0.0)
            part = _dot((u * u).astype(BF16), w2_ref[c0:c0 + fc, :])
            acc = part if acc is None else acc + part
        o_ref[rows, :] = x1 + gate2 * _row_rms(acc, g_post_ff_ref[...])


def _outffn_call(layer, ah, aa, x2d, mod, g_post_mix, g_pre_ff, g_post_ff, wo_bf, w1_bf, w2_bf,
                 seq, tm):
    rows, d = x2d.shape
    d_ff = w1_bf.shape[2]
    tiles_per_seq = seq // tm
    vec = pl.BlockSpec((None, 1, d), lambda i: (layer, 0, 0))

    def resident(shape):
        return pl.BlockSpec((None,) + shape, lambda i: (layer, 0, 0),
                            pipeline_mode=pl.Buffered(1))

    return pl.pallas_call(
        _outffn_kernel,
        grid=(rows // tm,),
        in_specs=[
            pl.BlockSpec((ah.shape[0], tm, LANES), lambda i: (0, i, 0)),
            pl.BlockSpec((aa.shape[0], tm, LANES), lambda i: (0, i, 0)),
            pl.BlockSpec((tm, d), lambda i: (i, 0)),
            pl.BlockSpec((None, None, N_MOD, d), lambda i: (layer, i // tiles_per_seq, 0, 0)),
            vec, vec, vec,
            resident((d, d)), resident((d, d_ff)), resident((d_ff, d)),
        ],
        out_specs=pl.BlockSpec((tm, d), lambda i: (i, 0)),
        out_shape=jax.ShapeDtypeStruct((rows, d), F32),
        compiler_params=pltpu.CompilerParams(
            dimension_semantics=("arbitrary",), vmem_limit_bytes=VMEM_LIMIT_BYTES),
        name="outffn",
    )(ah, aa, x2d, mod, g_post_mix, g_pre_ff, g_post_ff, wo_bf, w1_bf, w2_bf)


def kernel(x, c, w_ada, ada_layer_bias, hg_lb_logits, pre_mix, post_mix, w_in, w_out, hg_out_gain,
           q_gain, k_gain, att_out_gain, pre_ff, post_ff, w_ff1, w_ff2):
    nb, seq, d = x.shape
    depth = w_in.shape[0]
    assert w_in.shape[2] == N_COL_BLOCKS * LANES and d == HG_WIDTH + ATT_WIDTH
    tm = min(1024, seq)
    tm_in = min(512, seq)
    tq = min(512, seq)

    mod, lb_table = _cond_call(c, w_ada, ada_layer_bias, hg_lb_logits)
    mod = mod.reshape(depth, nb, N_MOD, d)
    lb_table = lb_table.reshape(depth, HG_HEADS, 1, LANES)
    cos, sin = _rope_tables(seq)

    def rows3(a):
        return a.reshape(depth, 1, a.shape[-1])

    def head_pair(g):
        return rows3(jnp.tile(g, (1, LANES // ATT_HEAD_DIM)))

    w_in_bf, w_out_bf = w_in.astype(BF16), w_out.astype(BF16)
    w_ff1_bf, w_ff2_bf = w_ff1.astype(BF16), w_ff2.astype(BF16)
    pre_mix, post_mix, pre_ff, post_ff = map(rows3, (pre_mix, post_mix, pre_ff, post_ff))
    hg_out_gain = rows3(hg_out_gain)
    q_gain, k_gain, att_out_gain = map(head_pair, (q_gain, k_gain, att_out_gain))

    x2d = x.reshape(nb * seq, d)
    for l in range(depth):
        z = _inproj_call(l, x2d, mod, pre_mix, w_in_bf, lb_table, seq, tm_in)
        z4 = z.reshape(N_OUT_BLOCKS, nb, seq, LANES)
        y_hgrn = _hgrn_call(l, z4, hg_out_gain)
        y_att = _attn_call(l, z4, cos, sin, q_gain, k_gain, att_out_gain, tq)
        x2d = _outffn_call(l, y_hgrn, y_att, x2d, mod, post_mix, pre_ff, post_ff,
                           w_out_bf, w_ff1_bf, w_ff2_bf, seq, tm)
    return x2d.reshape(nb, seq, d)
```

```python
import functools

import numpy as np
import jax
import jax.numpy as jnp
from jax import lax
from jax.experimental import pallas as pl
from jax.experimental.pallas import tpu as pltpu

F32 = jnp.float32
BF16 = jnp.bfloat16

LANES = 128
VMEM_LIMIT_BYTES = 56 * 1024 * 1024
ROW_PART = 256

EPS = 1e-6
N_MOD = 6
HG_HEADS = 4
HG_HEAD_DIM = 128
HG_WIDTH = HG_HEADS * HG_HEAD_DIM
HG_CHUNK = 64
HG_GROUP = 8
HG_PREP_ROWS = 256
LOG2E = 1.4426950408889634
HG_LEVEL_HALVES = (32, 16, 8, 4, 2, 1)
ATT_HEADS = 8
ATT_HEAD_DIM = 64
ATT_KV_HEADS = 2
ATT_WIDTH = ATT_HEADS * ATT_HEAD_DIM
KV_WIDTH = ATT_KV_HEADS * ATT_HEAD_DIM
GRID_W = 64
ROPE_THETA = 10000.0
ATT_SUB_ROWS = 256

CB_HQ, CB_FFW, CB_FBW, CB_HI, CB_HG = 0, 4, 8, 12, 16
CB_AQ, CB_AK, CB_AV = 20, 24, 25
N_COL_BLOCKS = 26
OB_QS, OB_K, OB_B, OB_V, OB_SG = 0, 4, 12, 20, 24
OB_AQ, OB_AK, OB_AV = 28, 32, 33
N_OUT_BLOCKS = 34
ATT_PAIRS = ATT_WIDTH // LANES
PAIRS_PER_KV = ATT_PAIRS // ATT_KV_HEADS


def _dot(a, b):
    return jnp.dot(a, b, preferred_element_type=F32)


def _dot_nt(a, b):
    return lax.dot_general(a, b, (((1,), (1,)), ((), ())), preferred_element_type=F32)


def _split2(x):
    hi = x.astype(BF16)
    lo = (x - hi.astype(F32)).astype(BF16)
    return hi, lo


def _split3(x):
    hi = x.astype(BF16)
    r1 = x - hi.astype(F32)
    mid = r1.astype(BF16)
    lo = (r1 - mid.astype(F32)).astype(BF16)
    return hi, mid, lo


def _sigmoid(x):
    t = jnp.exp(-jnp.abs(x))
    return jnp.where(x >= 0, 1.0, t) / (1.0 + t)


def _row_rms(x, gain):
    ms = jnp.mean(x * x, axis=-1, keepdims=True)
    return x * lax.rsqrt(ms + EPS) * gain


def _group_mean_sq(x, gmat):
    w = x.shape[1]
    r = _dot(jnp.concatenate(_split2(x * x), axis=1), gmat)
    return r[:, :w] + r[:, w:]


def _cond_kernel(c_ref, w_ref, bias_ref, lbl_ref, mod_ref, lb_ref):
    depth = bias_ref.shape[0]
    c = c_ref[...]
    s = c * _sigmoid(c)
    s_hi = s.astype(BF16).astype(F32)
    s_lo = s - s_hi
    w_hi, w_lo = _split2(w_ref[...])
    nb = c.shape[0]
    r1 = _dot(jnp.concatenate([s_hi, s_lo], axis=0).astype(BF16), w_hi)
    r2 = _dot(jnp.concatenate([s_hi, s_hi], axis=0).astype(BF16), w_lo)
    cond = r1[:nb] + r1[nb:] + r2[:nb]
    for l in range(depth):
        mod_ref[l] = cond + bias_ref[l:l + 1, :]

    rows = [lbl_ref[l:l + 1, :] for l in range(depth)]
    m = functools.reduce(jnp.maximum, rows)
    es = [jnp.exp(r - m) for r in rows]
    tot = functools.reduce(lambda a, b: a + b, es)
    cum = None
    first = None
    for l in range(depth):
        p = es[l] / tot
        cum = p if cum is None else cum + p
        if first is None:
            first = cum
        lb_ref[l:l + 1, :] = cum - first


def _cond_call(c, w_ada, ada_layer_bias, hg_lb_logits):
    nb, d = c.shape
    n_out = w_ada.shape[1]
    depth = ada_layer_bias.shape[0]
    tn = n_out // 4
    return pl.pallas_call(
        _cond_kernel,
        grid=(n_out // tn,),
        in_specs=[
            pl.BlockSpec((nb, d), lambda j: (0, 0)),
            pl.BlockSpec((d, tn), lambda j: (0, j)),
            pl.BlockSpec((depth, tn), lambda j: (0, j)),
            pl.BlockSpec(hg_lb_logits.shape, lambda j: (0, 0)),
        ],
        out_specs=[
            pl.BlockSpec((depth, nb, tn), lambda j: (0, 0, j)),
            pl.BlockSpec(hg_lb_logits.shape, lambda j: (0, 0)),
        ],
        out_shape=[
            jax.ShapeDtypeStruct((depth, nb, n_out), F32),
            jax.ShapeDtypeStruct(hg_lb_logits.shape, F32),
        ],
        compiler_params=pltpu.CompilerParams(
            dimension_semantics=("arbitrary",), vmem_limit_bytes=VMEM_LIMIT_BYTES),
        name="cond",
    )(c, w_ada, ada_layer_bias, hg_lb_logits)


def _silu(z):
    return z / (1.0 + jnp.exp2(z * -LOG2E))


def _hgrn_gate(zf, lb, one_m_lb):
    t = jnp.exp(-jnp.abs(zf))
    neg = zf < 0
    inv = 1.0 / (1.0 + t)
    num = jnp.where(neg, lb + t, lb * t + 1.0)
    g = jnp.where(num > 0, jnp.log(num * inv), zf) * LOG2E
    return one_m_lb * (jnp.where(neg, 1.0, t) * inv), g


def _inproj_kernel(x_ref, mod_ref, gain_ref, w_ref, lb_ref, tri_ref, z_ref):
    shift = mod_ref[0:1, :]
    scale = mod_ref[1:2, :]
    c = HG_CHUNK

    def emit(cb, rows, r0, z, decays):
        if cb >= CB_AQ:
            z_ref[OB_AQ + cb - CB_AQ, rows, :] = z
        elif cb >= CB_HG:
            z_ref[OB_SG + cb - CB_HG, rows, :] = _silu(z)
        elif cb >= CB_HI:
            z_ref[OB_V + cb - CB_HI, rows, :] = z
        elif cb >= CB_FFW:
            d, head = divmod(cb - CB_FFW, HG_HEADS)
            lb = lb_ref[head]
            k, g = _hgrn_gate(z, lb, 1.0 - lb)
            z_ref[OB_K + d * HG_HEADS + head, rows, :] = k
            decays.append((d, head, r0, _split3(g)))
        else:
            z_ref[OB_QS + cb - CB_HQ, rows, :] = _silu(z)

    def cumulate(decays):
        for d, head, r0, parts in decays:
            for jc in range(parts[0].shape[0] // c):
                g3 = jnp.concatenate([p[jc * c:(jc + 1) * c] for p in parts], axis=0)
                z_ref[OB_B + d * HG_HEADS + head, pl.ds(r0 + jc * c, c), :] = _dot(tri_ref[d], g3)

    part_rows = min(ROW_PART, x_ref.shape[0])
    pending = []
    for r0 in range(0, x_ref.shape[0], part_rows):
        rows = pl.ds(r0, part_rows)
        h = _row_rms(x_ref[rows, :], gain_ref[...]) * (1.0 + scale) + shift
        hb = h.astype(BF16)
        decays = []
        for cb in range(0, N_COL_BLOCKS, 2):
            z = _dot(hb, w_ref[:, cb * LANES:(cb + 2) * LANES])
            emit(cb, rows, r0, z[:, :LANES], decays)
            emit(cb + 1, rows, r0, z[:, LANES:], decays)
        cumulate(pending)
        pending = decays
    cumulate(pending)


def _inproj_call(layer, x2d, mod, gain, w_in_bf, lb, seq, tm):
    rows, d = x2d.shape
    n_cols = w_in_bf.shape[2]
    n_blocks = N_OUT_BLOCKS
    tiles_per_seq = seq // tm
    tri = _hgrn_tri()
    return pl.pallas_call(
        _inproj_kernel,
        grid=(rows // tm,),
        in_specs=[
            pl.BlockSpec((tm, d), lambda i: (i, 0)),
            pl.BlockSpec((None, None, N_MOD, d), lambda i: (layer, i // tiles_per_seq, 0, 0)),
            pl.BlockSpec((None, 1, d), lambda i: (layer, 0, 0)),
            pl.BlockSpec((None, d, n_cols), lambda i: (layer, 0, 0),
                         pipeline_mode=pl.Buffered(1)),
            pl.BlockSpec((None, HG_HEADS, 1, LANES), lambda i: (layer, 0, 0, 0)),
            pl.BlockSpec(tri.shape, lambda i: (0, 0, 0)),
        ],
        out_specs=pl.BlockSpec((n_blocks, tm, LANES), lambda i: (0, i, 0)),
        out_shape=jax.ShapeDtypeStruct((n_blocks, rows, LANES), F32),
        compiler_params=pltpu.CompilerParams(
            dimension_semantics=("arbitrary",), vmem_limit_bytes=VMEM_LIMIT_BYTES),
        name="inproj",
    )(x2d, mod, gain, w_in_bf, lb, jnp.asarray(tri, BF16))


def _hgrn_tri():
    t = np.arange(HG_CHUNK)[:, None]
    s = np.arange(HG_CHUNK)[None, :]
    return np.tile(np.stack([s <= t, s >= t]).astype(np.float32), (1, 1, 3))


def _hgrn_masks():
    c = HG_CHUNK
    t = np.arange(c)[:, None]
    s = np.arange(c)[None, :]
    fwd = []
    for half in HG_LEVEL_HALVES:
        same = (t // (2 * half)) == (s // (2 * half))
        t_low = (t // half) % 2 == 1
        s_up = (s // half) % 2 == 0
        fwd.append(same & t_low & s_up)
    fwd.append(t == s)
    fwd = np.stack(fwd).astype(np.float32)
    masks = np.stack([fwd, fwd.transpose(0, 2, 1)])
    return np.tile(masks, (1, 1, 1, 2))


def _hgrn_kernel(qs_ref, kfw_ref, bfw_ref, kbw_ref, bbw_ref, v_ref, sg_ref, gain_ref, mask_ref,
                 y_ref, st_scr, o_scr, v_scr, vt_scr):
    seq = qs_ref.shape[0]
    n_chunks = seq // HG_CHUNK
    c = HG_CHUNK
    grp = HG_GROUP
    slab = min(HG_PREP_ROWS, seq)
    n_sub = max(n for n in (1, 2, 4, 8) if seq % (n * slab) == 0)
    qs_scr = qs_ref
    k_scr = (kfw_ref, kbw_ref)
    b_scr = (bfw_ref, bbw_ref)
    row = lax.broadcasted_iota(jnp.int32, (c, LANES), 0)
    left = lax.broadcasted_iota(jnp.int32, (c, LANES), 1) < c
    sub = lax.broadcasted_iota(jnp.int32, (8, LANES), 0)

    st_scr[...] = jnp.zeros(st_scr.shape, F32)

    def prep(i, carry):
        for j in range(n_sub):
            rows = pl.ds(pl.multiple_of((i * n_sub + j) * slab, slab), slab)
            v = v_ref[rows, :]
            v_scr[rows, :] = v.astype(BF16)
            for jc in range(slab // c):
                vt_scr[(i * n_sub + j) * (slab // c) + jc] = v[jc * c:(jc + 1) * c].T.astype(BF16)
        return carry

    lax.fori_loop(0, seq // (slab * n_sub), prep, 0)

    def mid_reference(d, r0, b, half):
        def brow(r, n):
            return jnp.broadcast_to(b_scr[d][pl.ds(r0 + r, 1), :], (n, LANES))

        if half == c // 2:
            return brow(half, c)
        if half >= 4:
            return jnp.concatenate(
                [brow(base + half, 2 * half) for base in range(0, c, 2 * half)], axis=0)
        assert half == 2
        return jnp.concatenate(
            [jnp.where(sub < 4, brow(base + 2, 8), brow(base + 6, 8))
             for base in range(0, c, 8)], axis=0)

    def trip(i):
        chunk_ids = [i * grp + j for j in range(grp)]
        chunk_ids += [n_chunks - 1 - (i * grp + j) for j in range(grp)]
        insts = [(slot // grp, cid * c) for slot, cid in enumerate(chunk_ids)]
        pairs = [(d, (s, s + 1)) for d in range(2) for s in range(d * grp, (d + 1) * grp, 2)]

        qs_l = [qs_scr[pl.ds(r0, c), :] for _, r0 in insts]
        k_l = [k_scr[d][pl.ds(r0, c), :] for d, r0 in insts]
        v_l = [v_scr[pl.ds(r0, c), :] for _, r0 in insts]
        b_l = [b_scr[d][pl.ds(r0, c), :] for d, r0 in insts]
        qsb_l = [qs.astype(BF16) for qs in qs_l]
        kb_l = [k.astype(BF16) for k in k_l]

        o_l = [None] * len(insts)
        for d in range(2):
            st = st_scr[d]
            for slot in range(d * grp, (d + 1) * grp):
                edge = c - 1 if d == 0 else 0
                b_edge = b_scr[d][pl.ds(insts[slot][1] + edge, 1), :]
                qe = (qs_l[slot] * jnp.exp2(b_l[slot])).astype(BF16)
                kd = (k_l[slot] * jnp.exp2(b_edge - b_l[slot])).astype(BF16)
                o_l[slot] = _dot(qe, st.T.astype(BF16))
                st = st * jnp.exp2(b_edge) + _dot(vt_scr[chunk_ids[slot]], kd)
            st_scr[d] = st

        zero = jnp.zeros((c, LANES), BF16)

        def block_diag(xa, xb):
            return jnp.concatenate([jnp.concatenate([xa, zero], axis=1),
                                    jnp.concatenate([zero, xb], axis=1)], axis=0)

        def pair_scores(xa_q, xa_k, xb_q, xb_k):
            return _dot_nt(jnp.concatenate([xa_q, xb_q], axis=1), block_diag(xa_k, xb_k))

        n_lvl = len(HG_LEVEL_HALVES)
        diag_l, adj_l = [], []
        for slot, (d, _) in enumerate(insts):
            qs, k, b = qs_l[slot], k_l[slot], b_l[slot]
            shift = 1 if d == 0 else c - 1
            k_adj = pltpu.roll(k, shift, axis=0)
            decay = jnp.exp2(jnp.minimum(b - pltpu.roll(b, shift, axis=0), 0.0))
            diag_l.append(jnp.sum(qs * k, axis=-1, keepdims=True))
            adj_l.append(jnp.sum(qs * k_adj * decay, axis=-1, keepdims=True))
        sc_l = []
        for d, (sa, sb) in pairs:
            sc_l.append(jnp.where(left, diag_l[sa], diag_l[sb]) * mask_ref[d, n_lvl]
                        + jnp.where(left, adj_l[sa], adj_l[sb]) * mask_ref[d, n_lvl - 1])
        for lvl, half in enumerate(HG_LEVEL_HALVES[:-1]):
            for p, (d, slots) in enumerate(pairs):
                xs = []
                for slot in slots:
                    b = b_l[slot]
                    q_rows = ((row & half) != 0) if d == 0 else ((row & half) == 0)
                    sign = jnp.where(q_rows, 1.0, -1.0)
                    e = jnp.exp2((b - mid_reference(d, insts[slot][1], b, half)) * sign)
                    xs.append(jnp.where(q_rows, qsb_l[slot], kb_l[slot]) * e.astype(BF16))
                sc_l[p] = sc_l[p] + pair_scores(xs[0], xs[0], xs[1], xs[1]) * mask_ref[d, lvl]

        for p, (d, (sa, sb)) in enumerate(pairs):
            o_intra = _dot(sc_l[p].astype(BF16),
                           block_diag(v_l[sa], v_l[sb]))
            o_scr[d, pl.ds(insts[sa][1], c), :] = o_l[sa] + o_intra[:, :LANES]
            o_scr[d, pl.ds(insts[sb][1], c), :] = o_l[sb] + o_intra[:, LANES:]

    for i in range(n_chunks // grp):
        trip(i)

    gain = gain_ref[...]

    def fin(i, carry):
        for j in range(n_sub):
            rows = pl.ds(pl.multiple_of((i * n_sub + j) * slab, slab), slab)
            o = o_scr[0, rows, :] + o_scr[1, rows, :]
            y_ref[rows, :] = (_row_rms(o, gain) * sg_ref[rows, :]).astype(y_ref.dtype)
        return carry

    lax.fori_loop(0, seq // (slab * n_sub), fin, 0)


def _hgrn_call(layer, z4, out_gain):
    _, nb, seq, _ = z4.shape
    assert seq % (HG_GROUP * HG_CHUNK) == 0 and seq % HG_PREP_ROWS == 0
    masks = _hgrn_masks()

    def zspec(ob0):
        return pl.BlockSpec((None, None, seq, LANES), lambda b, h: (ob0 + h, b, 0, 0))

    y = pl.pallas_call(
        _hgrn_kernel,
        grid=(nb, HG_HEADS),
        in_specs=[
            zspec(OB_QS), zspec(OB_K), zspec(OB_B), zspec(OB_K + HG_HEADS), zspec(OB_B + HG_HEADS),
            zspec(OB_V), zspec(OB_SG),
            pl.BlockSpec((None, 1, LANES), lambda b, h: (layer, 0, 0)),
            pl.BlockSpec(masks.shape, lambda b, h: (0, 0, 0, 0)),
        ],
        out_specs=pl.BlockSpec((None, None, seq, LANES), lambda b, h: (h, b, 0, 0)),
        out_shape=jax.ShapeDtypeStruct((HG_HEADS, nb, seq, LANES), BF16),
        scratch_shapes=[
            pltpu.VMEM((2, HG_HEAD_DIM, HG_HEAD_DIM), F32),
            pltpu.VMEM((2, seq, LANES), F32),
            pltpu.VMEM((seq, LANES), BF16),
            pltpu.VMEM((seq // HG_CHUNK, HG_HEAD_DIM, HG_CHUNK), BF16),
        ],
        compiler_params=pltpu.CompilerParams(
            dimension_semantics=("arbitrary", "arbitrary"), vmem_limit_bytes=VMEM_LIMIT_BYTES),
        name="hgrn",
    )(z4, z4, z4, z4, z4, z4, z4, out_gain, jnp.asarray(masks, F32))
    return y.reshape(HG_HEADS, nb * seq, LANES)


def _rope_tables(seq):
    pos = jnp.arange(seq)
    row = (pos // GRID_W).astype(F32)
    col = (pos % GRID_W).astype(F32)
    half = ATT_HEAD_DIM // 2
    inv_freq = 1.0 / (ROPE_THETA ** (jnp.arange(0, half, 2, dtype=F32) / half))
    ang_r = row[:, None] * inv_freq
    ang_c = col[:, None] * inv_freq
    cos = jnp.concatenate([jnp.cos(ang_r)] * 2 + [jnp.cos(ang_c)] * 2, axis=-1)
    sin = jnp.concatenate([-jnp.sin(ang_r), jnp.sin(ang_r), -jnp.sin(ang_c), jnp.sin(ang_c)], axis=-1)
    reps = LANES // ATT_HEAD_DIM
    return jnp.tile(cos, (1, reps)), jnp.tile(sin, (1, reps))


def _rope(x, cos, sin, lane):
    quarter = ATT_HEAD_DIM // 4
    fwd = pltpu.roll(x, LANES - quarter, axis=1)
    bwd = pltpu.roll(x, quarter, axis=1)
    partner = jnp.where((lane & quarter) == 0, fwd, bwd)
    return x * cos + partner * sin


def _attn_kernel(q0_ref, q1_ref, q2_ref, q3_ref, k_ref, v_ref, cosq_ref, sinq_ref, cosk_ref, sink_ref,
                 qg_ref, kg_ref, og_ref, gmat_ref, y_ref, kext_scr, vext_scr):
    seq = k_ref.shape[0]
    tq = q0_ref.shape[0]
    gmat = gmat_ref[...]

    @pl.when(pl.program_id(1) == 0)
    def _prepare_kv():
        slab = min(512, seq)
        lane = lax.broadcasted_iota(jnp.int32, (slab, LANES), 1)
        first = lane < ATT_HEAD_DIM
        for r0 in range(0, seq, slab):
            rs = slice(r0, r0 + slab)
            k = k_ref[rs, :]
            kn = k * lax.rsqrt(_group_mean_sq(k, gmat) + EPS) * kg_ref[...]
            kr = _rope(kn, cosk_ref[rs, :], sink_ref[rs, :], lane)
            kr_sw = pltpu.roll(kr, ATT_HEAD_DIM, axis=1)
            kext_scr[0, rs, :] = jnp.where(first, kr, kr_sw).astype(BF16)
            kext_scr[1, rs, :] = jnp.where(first, kr_sw, kr).astype(BF16)
            v = v_ref[rs, :]
            v_sw = pltpu.roll(v, ATT_HEAD_DIM, axis=1)
            vext_scr[0, 0, rs, :] = jnp.where(first, v, 1.0).astype(BF16)
            vext_scr[0, 1, rs, :] = jnp.where(first, 1.0, v_sw).astype(BF16)
            vext_scr[1, 0, rs, :] = jnp.where(first, v_sw, 1.0).astype(BF16)
            vext_scr[1, 1, rs, :] = jnp.where(first, 1.0, v).astype(BF16)

    sub_rows = min(ATT_SUB_ROWS, tq)
    lane = lax.broadcasted_iota(jnp.int32, (sub_rows, LANES), 1)
    first = lane < ATT_HEAD_DIM
    sm_scale = ATT_HEAD_DIM ** -0.5 * LOG2E
    q_refs = (q0_ref, q1_ref, q2_ref, q3_ref)
    heads = [(r0, pair, parity) for r0 in range(0, tq, sub_rows)
             for pair in range(ATT_PAIRS) for parity in range(2)]

    def scores(r0, pair, parity):
        if parity == 0:
            rows = pl.ds(r0, sub_rows)
            q = q_refs[pair][rows, :]
            qn = q * lax.rsqrt(_group_mean_sq(q, gmat) + EPS) * (qg_ref[...] * sm_scale)
            scores.qr = _rope(qn, cosq_ref[rows, :], sinq_ref[rows, :], lane)
        keep = first if parity == 0 else jnp.logical_not(first)
        qm = jnp.where(keep, scores.qr, 0.0).astype(BF16)
        return _dot_nt(qm, kext_scr[pair // PAIRS_PER_KV])

    halves = []

    def values(r0, pair, parity, p):
        ov = _dot(p, vext_scr[pair // PAIRS_PER_KV, parity])
        halves.append(ov / pltpu.roll(ov, ATT_HEAD_DIM, axis=1))
        if parity == 1:
            o = jnp.where(first, halves[0], halves[1])
            y = o * lax.rsqrt(_group_mean_sq(o, gmat) + EPS) * og_ref[...]
            y_ref[pair, pl.ds(r0, sub_rows), :] = y.astype(y_ref.dtype)
            halves.clear()

    s = scores(*heads[0])
    pending = None
    for idx, head in enumerate(heads):
        s_next = scores(*heads[idx + 1]) if idx + 1 < len(heads) else None
        if pending is not None:
            values(*pending)
        m = jnp.max(s, axis=-1, keepdims=True)
        pending = (*head, jnp.exp2(s - m).astype(BF16))
        s = s_next
    values(*pending)


def _attn_call(layer, z4, cos, sin, q_gain, k_gain, out_gain, tq):
    _, nb, seq, _ = z4.shape
    nq = seq // tq
    lanes2 = np.arange(2 * LANES)
    grp = (lanes2[:, None] // ATT_HEAD_DIM) == (lanes2[None, :] // ATT_HEAD_DIM)
    gmat = jnp.asarray(grp.astype(np.float32) / ATT_HEAD_DIM, BF16)

    def qspec(pair):
        return pl.BlockSpec((None, None, tq, LANES), lambda b, i: (OB_AQ + pair, b, i, 0))

    def full(cb):
        return pl.BlockSpec((None, None, seq, LANES), lambda b, i: (cb, b, 0, 0))

    vec = pl.BlockSpec((None, 1, LANES), lambda b, i: (layer, 0, 0))
    return pl.pallas_call(
        _attn_kernel,
        grid=(nb, nq),
        in_specs=[
            qspec(0), qspec(1), qspec(2), qspec(3), full(OB_AK), full(OB_AV),
            pl.BlockSpec((tq, LANES), lambda b, i: (i, 0)),
            pl.BlockSpec((tq, LANES), lambda b, i: (i, 0)),
            pl.BlockSpec((seq, LANES), lambda b, i: (0, 0)),
            pl.BlockSpec((seq, LANES), lambda b, i: (0, 0)),
            vec, vec, vec,
            pl.BlockSpec((2 * LANES, 2 * LANES), lambda b, i: (0, 0)),
        ],
        out_specs=pl.BlockSpec((ATT_PAIRS, tq, LANES), lambda b, i: (0, b * nq + i, 0)),
        out_shape=jax.ShapeDtypeStruct((ATT_PAIRS, nb * seq, LANES), BF16),
        scratch_shapes=[
            pltpu.VMEM((ATT_KV_HEADS, seq, LANES), BF16),
            pltpu.VMEM((ATT_KV_HEADS, 2, seq, LANES), BF16),
        ],
        compiler_params=pltpu.CompilerParams(
            dimension_semantics=("arbitrary", "arbitrary"), vmem_limit_bytes=VMEM_LIMIT_BYTES),
        name="attn",
    )(z4, z4, z4, z4, z4, z4, cos, sin, cos, sin, q_gain, k_gain, out_gain, gmat)


def _outffn_kernel(ah_ref, aa_ref, x_ref, mod_ref, g_post_mix_ref, g_pre_ff_ref, g_post_ff_ref,
                   wo_ref, w1_ref, w2_ref, o_ref):
    gate1 = mod_ref[2:3, :]
    shift2 = mod_ref[3:4, :]
    scale2 = mod_ref[4:5, :]
    gate2 = mod_ref[5:6, :]
    d_ff = w1_ref.shape[1]
    fc = 1024
    part_rows = min(ROW_PART, x_ref.shape[0])
    parts = [pl.ds(r0, part_rows) for r0 in range(0, x_ref.shape[0], part_rows)]
    ys = []
    for rows in parts:
        a = jnp.concatenate([ah_ref[j, rows, :] for j in range(ah_ref.shape[0])]
                            + [aa_ref[j, rows, :] for j in range(aa_ref.shape[0])], axis=1)
        ys.append(_dot(a, wo_ref[...]))
    for rows, y in zip(parts, ys):
        x1 = x_ref[rows, :] + gate1 * _row_rms(y, g_post_mix_ref[...])
        h = (_row_rms(x1, g_pre_ff_ref[...]) * (1.0 + scale2) + shift2).astype(BF16)
        acc = None
        for c0 in range(0, d_ff, fc):
            u = jnp.maximum(_dot(h, w1_ref[:, c0:c0 + fc]), 0.0)
            part = _dot((u * u).astype(BF16), w2_ref[c0:c0 + fc, :])
            acc = part if acc is None else acc + part
        o_ref[rows, :] = x1 + gate2 * _row_rms(acc, g_post_ff_ref[...])


def _outffn_call(layer, ah, aa, x2d, mod, g_post_mix, g_pre_ff, g_post_ff, wo_bf, w1_bf, w2_bf,
                 seq, tm):
    rows, d = x2d.shape
    d_ff = w1_bf.shape[2]
    tiles_per_seq = seq // tm
    vec = pl.BlockSpec((None, 1, d), lambda i: (layer, 0, 0))

    def resident(shape):
        return pl.BlockSpec((None,) + shape, lambda i: (layer, 0, 0),
                            pipeline_mode=pl.Buffered(1))

    return pl.pallas_call(
        _outffn_kernel,
        grid=(rows // tm,),
        in_specs=[
            pl.BlockSpec((ah.shape[0], tm, LANES), lambda i: (0, i, 0)),
            pl.BlockSpec((aa.shape[0], tm, LANES), lambda i: (0, i, 0)),
            pl.BlockSpec((tm, d), lambda i: (i, 0)),
            pl.BlockSpec((None, None, N_MOD, d), lambda i: (layer, i // tiles_per_seq, 0, 0)),
            vec, vec, vec,
            resident((d, d)), resident((d, d_ff)), resident((d_ff, d)),
        ],
        out_specs=pl.BlockSpec((tm, d), lambda i: (i, 0)),
        out_shape=jax.ShapeDtypeStruct((rows, d), F32),
        compiler_params=pltpu.CompilerParams(
            dimension_semantics=("arbitrary",), vmem_limit_bytes=VMEM_LIMIT_BYTES),
        name="outffn",
    )(ah, aa, x2d, mod, g_post_mix, g_pre_ff, g_post_ff, wo_bf, w1_bf, w2_bf)


def kernel(x, c, w_ada, ada_layer_bias, hg_lb_logits, pre_mix, post_mix, w_in, w_out, hg_out_gain,
           q_gain, k_gain, att_out_gain, pre_ff, post_ff, w_ff1, w_ff2):
    nb, seq, d = x.shape
    depth = w_in.shape[0]
    assert w_in.shape[2] == N_COL_BLOCKS * LANES and d == HG_WIDTH + ATT_WIDTH
    tm = min(1024, seq)
    tm_in = min(512, seq)
    tq = min(512, seq)

    mod, lb_table = _cond_call(c, w_ada, ada_layer_bias, hg_lb_logits)
    mod = mod.reshape(depth, nb, N_MOD, d)
    lb_table = lb_table.reshape(depth, HG_HEADS, 1, LANES)
    cos, sin = _rope_tables(seq)

    def rows3(a):
        return a.reshape(depth, 1, a.shape[-1])

    def head_pair(g):
        return rows3(jnp.tile(g, (1, LANES // ATT_HEAD_DIM)))

    w_in_bf, w_out_bf = w_in.astype(BF16), w_out.astype(BF16)
    w_ff1_bf, w_ff2_bf = w_ff1.astype(BF16), w_ff2.astype(BF16)
    pre_mix, post_mix, pre_ff, post_ff = map(rows3, (pre_mix, post_mix, pre_ff, post_ff))
    hg_out_gain = rows3(hg_out_gain)
    q_gain, k_gain, att_out_gain = map(head_pair, (q_gain, k_gain, att_out_gain))

    x2d = x.reshape(nb * seq, d)
    for l in range(depth):
        z = _inproj_call(l, x2d, mod, pre_mix, w_in_bf, lb_table, seq, tm_in)
        z4 = z.reshape(N_OUT_BLOCKS, nb, seq, LANES)
        y_hgrn = _hgrn_call(l, z4, hg_out_gain)
        y_att = _attn_call(l, z4, cos, sin, q_gain, k_gain, att_out_gain, tq)
        x2d = _outffn_call(l, y_hgrn, y_att, x2d, mod, post_mix, pre_ff, post_ff,
                           w_out_bf, w_ff1_bf, w_ff2_bf, seq, tm)
    return x2d.reshape(nb, seq, d)
```
